```python
import math
import jax, jax.numpy as jnp
from jax import lax
import numpy as np

D_MODEL = 1024
BATCH = 32
SEQ = 2048
DEPTH = 4

N_EVEN = (DEPTH + 1) // 2
N_ODD = DEPTH // 2
EPS = 1e-6

ATTN_WIDTH = D_MODEL // 2
CONV_WIDTH = D_MODEL // 2
DA_HEADS = 4
DA_HEAD_DIM = ATTN_WIDTH // (2 * DA_HEADS)
DA_V_DIM = 2 * DA_HEAD_DIM
Q_BLOCK = 128
CONV_KERNEL = 31
CONV_GROUPS = 8
AB_IN_WIDTH = 3 * ATTN_WIDTH + 2 * CONV_WIDTH

HGRN_HEADS = 8
HGRN_KDIM = D_MODEL // HGRN_HEADS
HGRN_VDIM = D_MODEL // HGRN_HEADS
HGRN_CHUNK = 32
HGRN_IN_WIDTH = 4 * D_MODEL

FFN_DENSE = 2816
N_EXPERTS = 8
TOP_K = 2
FFN_EXPERT = 3584

kernel_name = 'hybrid_diffattn_conformer_hgrn2_moe'


def rms_norm(x, gain):
    xf = x.astype(jnp.float32)
    y = xf * lax.rsqrt(jnp.mean(xf * xf, axis=-1, keepdims=True) + EPS)
    return (y * gain.astype(jnp.float32)).astype(x.dtype)


def swiglu(h, w_in, w_out):
    gate, up = jnp.split(h @ w_in, 2, axis=-1)
    return (jax.nn.silu(gate) * up) @ w_out


def diff_attention(q, k, v, q_gain, k_gain, lam_vecs, subln_gain, lam_init):
    b, s, _ = q.shape
    q = rms_norm(q.reshape(b, s, DA_HEADS, 2, DA_HEAD_DIM), q_gain).transpose(0, 2, 3, 1, 4)
    k = rms_norm(k.reshape(b, s, DA_HEADS, 2, DA_HEAD_DIM), k_gain).transpose(0, 2, 3, 1, 4)
    v = v.reshape(b, s, DA_HEADS, DA_V_DIM).transpose(0, 2, 1, 3)
    lv = lam_vecs.astype(jnp.float32)
    lam = jnp.exp(jnp.sum(lv[0] * lv[1])) - jnp.exp(jnp.sum(lv[2] * lv[3])) + lam_init
    scale = DA_HEAD_DIM ** -0.5
    outs = []
    for blk in range(s // Q_BLOCK):
        q0 = blk * Q_BLOCK
        q1 = q0 + Q_BLOCK
        scores = jnp.einsum('bhcqd,bhckd->bhcqk', q[:, :, :, q0:q1], k[:, :, :, :q1],
                            preferred_element_type=jnp.float32) * scale
        causal = (q0 + jnp.arange(Q_BLOCK))[:, None] >= jnp.arange(q1)[None, :]
        probs = jax.nn.softmax(jnp.where(causal, scores, -jnp.inf), axis=-1)
        weights = probs[:, :, 0] - lam * probs[:, :, 1]
        outs.append(jnp.einsum('bhqk,bhkv->bhqv', weights.astype(v.dtype), v[:, :, :q1]))
    o = rms_norm(jnp.concatenate(outs, axis=2), subln_gain) * (1.0 - lam_init)
    return o.transpose(0, 2, 1, 3).reshape(b, s, ATTN_WIDTH)


def conformer_conv(u, dw, dw_bias, gn_gain, gn_bias):
    b, s, _ = u.shape
    val, gate = jnp.split(u, 2, axis=-1)
    g = val * jax.nn.sigmoid(gate)
    y = lax.conv_general_dilated(g, dw[:, None, :].astype(g.dtype), window_strides=(1,),
                                 padding=[(CONV_KERNEL - 1, 0)],
                                 dimension_numbers=('NWC', 'WIO', 'NWC'),
                                 feature_group_count=CONV_WIDTH) + dw_bias
    yf = y.astype(jnp.float32).reshape(b, s, CONV_GROUPS, CONV_WIDTH // CONV_GROUPS)
    mean = jnp.mean(yf, axis=-1, keepdims=True)
    var = jnp.mean(jnp.square(yf - mean), axis=-1, keepdims=True)
    yn = ((yf - mean) * lax.rsqrt(var + EPS)).reshape(b, s, CONV_WIDTH) * gn_gain + gn_bias
    return jax.nn.silu(yn).astype(u.dtype)


def hgrn2(z, lower_bound, norm_gain):
    b, s, _ = z.shape
    q, f, i, g = jnp.split(z, 4, axis=-1)
    log_f = jnp.logaddexp(jnp.log(lower_bound),
                          jnp.log1p(-lower_bound) + jax.nn.log_sigmoid(f.astype(jnp.float32)))
    k = -jnp.expm1(log_f)
    n = s // HGRN_CHUNK

    def to_chunks(t, d):
        return t.astype(jnp.float32).reshape(b, n, HGRN_CHUNK, HGRN_HEADS, d).transpose(1, 0, 3, 2, 4)

    qc = to_chunks(jax.nn.silu(q), HGRN_KDIM)
    kc = to_chunks(k, HGRN_KDIM)
    vc = to_chunks(i, HGRN_VDIM)
    gc = to_chunks(log_f, HGRN_KDIM)
    causal = jnp.tril(jnp.ones((HGRN_CHUNK, HGRN_CHUNK), dtype=bool))

    def step(state, inp):
        qb, kb, vb, gb = inp
        cum = jnp.cumsum(gb, axis=2)
        last = cum[:, :, -1:, :]
        inter = jnp.einsum('bhtk,bhkv->bhtv', qb * jnp.exp(cum), state)
        rel = cum[:, :, :, None, :] - cum[:, :, None, :, :]
        decay = jnp.exp(jnp.where(causal[:, :, None], rel, -jnp.inf))
        scores = jnp.einsum('bhtk,bhsk,bhtsk->bhts', qb, kb, decay)
        out = inter + jnp.einsum('bhts,bhsv->bhtv', scores, vb)
        state = jnp.exp(last[:, :, 0, :])[..., None] * state + \
            jnp.einsum('bhsk,bhsv->bhkv', kb * jnp.exp(last - cum), vb)
        return state, out

    s0 = jnp.zeros((b, HGRN_HEADS, HGRN_KDIM, HGRN_VDIM), jnp.float32)
    _, o = lax.scan(step, s0, (qc, kc, vc, gc))
    o = o.transpose(1, 0, 3, 2, 4).reshape(b, s, HGRN_HEADS, HGRN_VDIM)
    o = rms_norm(o, norm_gain.reshape(HGRN_HEADS, HGRN_VDIM)).reshape(b, s, D_MODEL)
    return (o * jax.nn.silu(g.astype(jnp.float32))).astype(z.dtype)


def moe_swiglu(h, router, w_in, w_out):
    b, s, d = h.shape
    t = h.reshape(b * s, d)
    logits = (t @ router).astype(jnp.float32)
    top_vals, top_idx = lax.top_k(logits, TOP_K)
    gates = jax.nn.softmax(top_vals, axis=-1)
    combine = jnp.sum(jax.nn.one_hot(top_idx, N_EXPERTS, dtype=jnp.float32) * gates[..., None], axis=1)
    y = jnp.zeros_like(t)
    for e in range(N_EXPERTS):
        y = y + combine[:, e:e + 1].astype(t.dtype) * swiglu(t, w_in[e], w_out[e])
    return y.reshape(b, s, d)


def setup_inputs(seed: int = 0) -> dict:
    key = jax.random.key(seed)
    ks = jax.random.split(key, 22)

    def normal(k, shape, scale):
        return jax.random.normal(k, shape, jnp.float32) * scale

    def gain(k, shape):
        return 1.0 + normal(k, shape, 0.02)

    return {
        'x': normal(ks[0], (BATCH, SEQ, D_MODEL), 1.0),
        'mix_norm': gain(ks[1], (DEPTH, D_MODEL)),
        'ffn_norm': gain(ks[2], (DEPTH, D_MODEL)),
        'ab_w_in': normal(ks[3], (N_EVEN, D_MODEL, AB_IN_WIDTH), D_MODEL ** -0.5),
        'ab_w_out': normal(ks[4], (N_EVEN, ATTN_WIDTH + CONV_WIDTH, D_MODEL), (ATTN_WIDTH + CONV_WIDTH) ** -0.5),
        'da_q_gain': gain(ks[5], (N_EVEN, DA_HEAD_DIM)),
        'da_k_gain': gain(ks[6], (N_EVEN, DA_HEAD_DIM)),
        'da_lambda': normal(ks[7], (N_EVEN, 4, DA_HEAD_DIM), 0.1),
        'da_subln': gain(ks[8], (N_EVEN, DA_V_DIM)),
        'conv_dw': normal(ks[9], (N_EVEN, CONV_KERNEL, CONV_WIDTH), CONV_KERNEL ** -0.5),
        'conv_dw_bias': normal(ks[10], (N_EVEN, CONV_WIDTH), 0.01),
        'conv_gn_gain': gain(ks[11], (N_EVEN, CONV_WIDTH)),
        'conv_gn_bias': normal(ks[12], (N_EVEN, CONV_WIDTH), 0.01),
        'hgrn_w_in': normal(ks[13], (N_ODD, D_MODEL, HGRN_IN_WIDTH), D_MODEL ** -0.5),
        'hgrn_lb_logits': normal(ks[14], (N_ODD, HGRN_HEADS * HGRN_KDIM), 0.5),
        'hgrn_norm': gain(ks[15], (N_ODD, D_MODEL)),
        'hgrn_w_out': normal(ks[16], (N_ODD, D_MODEL, D_MODEL), D_MODEL ** -0.5),
        'ffn_w_in': normal(ks[17], (N_EVEN, D_MODEL, 2 * FFN_DENSE), D_MODEL ** -0.5),
        'ffn_w_out': normal(ks[18], (N_EVEN, FFN_DENSE, D_MODEL), FFN_DENSE ** -0.5),
        'moe_router': normal(ks[19], (N_ODD, D_MODEL, N_EXPERTS), D_MODEL ** -0.5),
        'moe_w_in': normal(ks[20], (N_ODD, N_EXPERTS, D_MODEL, 2 * FFN_EXPERT), D_MODEL ** -0.5),
        'moe_w_out': normal(ks[21], (N_ODD, N_EXPERTS, FFN_EXPERT, D_MODEL), FFN_EXPERT ** -0.5),
    }


def reference(x, mix_norm, ffn_norm, ab_w_in, ab_w_out, da_q_gain, da_k_gain, da_lambda, da_subln,
              conv_dw, conv_dw_bias, conv_gn_gain, conv_gn_bias, hgrn_w_in, hgrn_lb_logits, hgrn_norm,
              hgrn_w_out, ffn_w_in, ffn_w_out, moe_router, moe_w_in, moe_w_out):
    lb_all = jnp.cumsum(jax.nn.softmax(hgrn_lb_logits.astype(jnp.float32), axis=0), axis=0)
    lb_all = lb_all - lb_all[0:1]
    for layer in range(DEPTH):
        j = layer // 2
        h = rms_norm(x, mix_norm[layer])
        if layer % 2 == 0:
            z = h @ ab_w_in[j]
            q, k, v, u = jnp.split(z, [ATTN_WIDTH, 2 * ATTN_WIDTH, 3 * ATTN_WIDTH], axis=-1)
            lam_init = 0.8 - 0.6 * math.exp(-0.3 * layer)
            a = diff_attention(q, k, v, da_q_gain[j], da_k_gain[j], da_lambda[j], da_subln[j], lam_init)
            c = conformer_conv(u, conv_dw[j], conv_dw_bias[j], conv_gn_gain[j], conv_gn_bias[j])
            x = x + jnp.concatenate([a, c], axis=-1) @ ab_w_out[j]
            h = rms_norm(x, ffn_norm[layer])
            x = x + swiglu(h, ffn_w_in[j], ffn_w_out[j])
        else:
            o = hgrn2(h @ hgrn_w_in[j], lb_all[j], hgrn_norm[j])
            x = x + o @ hgrn_w_out[j]
            h = rms_norm(x, ffn_norm[layer])
            x = x + moe_swiglu(h, moe_router[j], moe_w_in[j], moe_w_out[j])
    return x
```

```python
import functools
import math

import numpy as np
import jax
import jax.numpy as jnp
from jax import lax
from jax.experimental import pallas as pl
from jax.experimental.pallas import tpu as pltpu

F32 = jnp.float32
BF16 = jnp.bfloat16

EPS = 1e-6
LANES = 128
SUBLANES = 8
VMEM_LIMIT = 48 * 1024 * 1024

DA_HEADS = 4
DA_HEAD_DIM = 64
CONV_KERNEL = 31
CONV_HALO = 32
CONV_GROUPS = 8
HGRN_HEADS = 8
HGRN_CHUNK = 128
N_EXPERTS = 8
TOP_K = 2


def _cparams(*sem):
    return pltpu.CompilerParams(dimension_semantics=sem, vmem_limit_bytes=VMEM_LIMIT)


def _rms(x, gain):
    ms = jnp.mean(x * x, axis=-1, keepdims=True)
    return x * lax.rsqrt(ms + EPS) * gain


def _silu(x):
    return x * jax.nn.sigmoid(x)


def _split3(x):
    a = x.astype(BF16)
    r = x - a.astype(F32)
    b = r.astype(BF16)
    c = (r - b.astype(F32)).astype(BF16)
    return a, b, c


def _dot(a, b):
    return jnp.dot(a, b, preferred_element_type=F32)


def _dot_nt(a, b):
    return lax.dot_general(a, b, (((1,), (1,)), ((), ())), preferred_element_type=F32)


def _dot_tn(a, b):
    return lax.dot_general(a, b, (((0,), (0,)), ((), ())), preferred_element_type=F32)


def _norm_matmul_kernel(x_ref, g_ref, w_ref, o_ref, h_ref):
    @pl.when(pl.program_id(1) == 0)
    def _():
        h_ref[...] = _rms(x_ref[...], g_ref[...]).astype(BF16)

    o_ref[...] = _dot(h_ref[...], w_ref[...]).astype(o_ref.dtype)


def norm_matmul(x, gain, w, *, tm, tn):
    t, d = x.shape
    n = w.shape[1]
    return pl.pallas_call(
        _norm_matmul_kernel,
        grid=(t // tm, n // tn),
        in_specs=[
            pl.BlockSpec((tm, d), lambda i, j: (i, 0)),
            pl.BlockSpec((1, d), lambda i, j: (0, 0)),
            pl.BlockSpec((d, tn), lambda i, j: (0, j)),
        ],
        out_specs=pl.BlockSpec((tm, tn), lambda i, j: (i, j)),
        out_shape=jax.ShapeDtypeStruct((t, n), BF16),
        scratch_shapes=[pltpu.VMEM((tm, d), BF16)],
        compiler_params=_cparams("parallel", "arbitrary"),
        name="norm_matmul",
    )(x, gain.reshape(1, d), w)


def _proj_residual_kernel(*refs):
    n_in = (len(refs) - 2) // 2
    res_ref, o_ref = refs[2 * n_in], refs[2 * n_in + 1]
    acc = res_ref[...]
    for i in range(n_in):
        acc = acc + _dot(refs[i][...], refs[n_in + i][...])
    o_ref[...] = acc


def proj_residual(acts, weights, res, *, tm):
    t, d = res.shape
    in_specs = [pl.BlockSpec((tm, a.shape[1]), lambda i: (i, 0)) for a in acts]
    in_specs += [pl.BlockSpec(w.shape, lambda i: (0, 0)) for w in weights]
    in_specs += [pl.BlockSpec((tm, d), lambda i: (i, 0))]
    return pl.pallas_call(
        _proj_residual_kernel,
        grid=(t // tm,),
        in_specs=in_specs,
        out_specs=pl.BlockSpec((tm, d), lambda i: (i, 0)),
        out_shape=jax.ShapeDtypeStruct((t, d), F32),
        compiler_params=_cparams("parallel"),
        name="proj_residual",
    )(*acts, *weights, res)


def _swiglu_kernel(eid_ref, x_ref, g_ref, wg_ref, wu_ref, wo_ref, o_ref, h_ref, acc_ref, *, dense):
    del eid_ref
    c = pl.program_id(1)

    @pl.when(c == 0)
    def _():
        if dense:
            h_ref[...] = _rms(x_ref[...], g_ref[...]).astype(BF16)
        else:
            h_ref[...] = x_ref[...]
        acc_ref[...] = jnp.zeros_like(acc_ref)

    h = h_ref[...]
    act = _silu(_dot(h, wg_ref[...])) * _dot(h, wu_ref[...])
    acc_ref[...] += _dot(act.astype(BF16), wo_ref[...])

    @pl.when(c == pl.num_programs(1) - 1)
    def _():
        if dense:
            o_ref[...] = x_ref[...] + acc_ref[...]
        else:
            o_ref[...] = acc_ref[...]


def swiglu(x, gain, w_in, w_out, block_expert, *, tm, fc, dense):
    t, d = x.shape
    f = w_out.shape[1]
    nf = f // fc
    grid_spec = pltpu.PrefetchScalarGridSpec(
        num_scalar_prefetch=1,
        grid=(t // tm, nf),
        in_specs=[
            pl.BlockSpec((tm, d), lambda i, c, eid: (i, 0)),
            pl.BlockSpec((1, d), lambda i, c, eid: (0, 0)),
            pl.BlockSpec((None, d, fc), lambda i, c, eid: (eid[i], 0, c)),
            pl.BlockSpec((None, d, fc), lambda i, c, eid: (eid[i], 0, c + nf)),
            pl.BlockSpec((None, fc, d), lambda i, c, eid: (eid[i], c, 0)),
        ],
        out_specs=pl.BlockSpec((tm, d), lambda i, c, eid: (i, 0)),
        scratch_shapes=[pltpu.VMEM((tm, d), BF16), pltpu.VMEM((tm, d), F32)],
    )
    return pl.pallas_call(
        functools.partial(_swiglu_kernel, dense=dense),
        grid_spec=grid_spec,
        out_shape=jax.ShapeDtypeStruct((t, d), F32),
        compiler_params=_cparams("parallel", "arbitrary"),
        name="swiglu_dense" if dense else "swiglu_experts",
    )(block_expert, x, gain.reshape(1, d), w_in, w_in, w_out)


def _half_rms(x, gain, low):
    sq = x * x
    s_lo = jnp.sum(jnp.where(low, sq, 0.0), axis=-1, keepdims=True)
    s_hi = jnp.sum(jnp.where(low, 0.0, sq), axis=-1, keepdims=True)
    ms = jnp.where(low, s_lo, s_hi) * (1.0 / DA_HEAD_DIM)
    return x * lax.rsqrt(ms + EPS) * gain


def _diff_attn_kernel(lam_ref, q_ref, k_ref, v_ref, qg_ref, kg_ref, sg_ref, o_ref, kn_ref,
                      *, tq, lam_init):
    qi = pl.program_id(2)
    low = lax.broadcasted_iota(jnp.int32, (1, LANES), 1) < DA_HEAD_DIM

    @pl.when(qi == 0)
    def _():
        kn_ref[...] = _half_rms(k_ref[...].astype(F32), kg_ref[...], low).astype(BF16)

    scale = DA_HEAD_DIM ** -0.5
    qn = _half_rms(q_ref[...].astype(F32), qg_ref[...], low) * scale
    q_maps = (jnp.where(low, qn, 0.0).astype(BF16), jnp.where(low, 0.0, qn).astype(BF16))

    def block(j, carry, masked):
        start = pl.multiple_of(j * tq, tq)
        kb = kn_ref[pl.ds(start, tq), :]
        vb = v_ref[pl.ds(start, tq), :]
        out = []
        for c in range(2):
            m, l, acc = carry[c]
            s = _dot_nt(q_maps[c], kb)
            if masked:
                row = lax.broadcasted_iota(jnp.int32, (tq, tq), 0)
                col = lax.broadcasted_iota(jnp.int32, (tq, tq), 1)
                s = jnp.where(row >= col, s, -jnp.inf)
            m_new = jnp.maximum(m, jnp.max(s, axis=-1, keepdims=True))
            alpha = jnp.exp(m - m_new)
            p = jnp.exp(s - m_new)
            l = alpha * l + jnp.sum(p, axis=-1, keepdims=True)
            acc = alpha * acc + _dot(p.astype(BF16), vb)
            out.append((m_new, l, acc))
        return tuple(out)

    init = tuple((jnp.full((tq, 1), -jnp.inf, F32), jnp.zeros((tq, 1), F32),
                  jnp.zeros((tq, LANES), F32)) for _ in range(2))
    carry = lax.fori_loop(0, qi, lambda j, c: block(j, c, False), init)
    (_, l0, a0), (_, l1, a1) = block(qi, carry, True)

    lam = lam_ref[0]
    o = a0 / l0 - lam * (a1 / l1)
    o_ref[...] = (_rms(o, sg_ref[...]) * (1.0 - lam_init)).astype(o_ref.dtype)


def diff_attention(z, lam, q_gain, k_gain, subln_gain, *, seq, tq, lam_init):
    t = z.shape[0]
    b = t // seq
    nq = seq // tq
    qg = jnp.concatenate([q_gain, q_gain]).reshape(1, LANES)
    kg = jnp.concatenate([k_gain, k_gain]).reshape(1, LANES)
    sg = subln_gain.reshape(1, LANES)
    vec = pl.BlockSpec((1, LANES), lambda bi, h, qi: (0, 0))
    return pl.pallas_call(
        functools.partial(_diff_attn_kernel, tq=tq, lam_init=lam_init),
        grid=(b, DA_HEADS, nq),
        in_specs=[
            pl.BlockSpec(memory_space=pltpu.SMEM),
            pl.BlockSpec((tq, LANES), lambda bi, h, qi: (bi * nq + qi, h)),
            pl.BlockSpec((seq, LANES), lambda bi, h, qi: (bi, DA_HEADS + h)),
            pl.BlockSpec((seq, LANES), lambda bi, h, qi: (bi, 2 * DA_HEADS + h)),
            vec, vec, vec,
        ],
        out_specs=pl.BlockSpec((tq, LANES), lambda bi, h, qi: (bi * nq + qi, h)),
        out_shape=jax.ShapeDtypeStruct((t, DA_HEADS * LANES), BF16),
        scratch_shapes=[pltpu.VMEM((seq, LANES), BF16)],
        compiler_params=_cparams("parallel", "parallel", "arbitrary"),
        name="diff_attention",
    )(lam.reshape(1), z, z, z, qg, kg, sg)


def _conv_kernel(val_ref, gate_ref, pval_ref, pgate_ref, dw_ref, b_ref, avg_ref, gg_ref, gb_ref,
                 o_ref, g_ref, *, tr):
    ti = pl.program_id(1)
    halo = pval_ref[...].astype(F32) * jax.nn.sigmoid(pgate_ref[...].astype(F32))
    g_ref[0:CONV_HALO, :] = jnp.where(ti == 0, 0.0, halo)
    g_ref[CONV_HALO:, :] = val_ref[...].astype(F32) * jax.nn.sigmoid(gate_ref[...].astype(F32))

    first = CONV_HALO - (CONV_KERNEL - 1)
    y = jnp.broadcast_to(b_ref[...], o_ref.shape)
    for k in range(CONV_KERNEL):
        y = y + g_ref[first + k:first + k + tr, :] * dw_ref[k:k + 1, :]

    avg = avg_ref[...]

    def group_mean(u):
        a, b, c = _split3(u)
        return _dot(a, avg) + _dot(b, avg) + _dot(c, avg)

    d = y - group_mean(y)
    var = group_mean(d * d)
    yn = d * lax.rsqrt(var + EPS) * gg_ref[...] + gb_ref[...]
    o_ref[...] = _silu(yn).astype(o_ref.dtype)


def conformer_conv(z, dw, dw_bias, gn_gain, gn_bias, *, seq, tr, col0):
    t = z.shape[0]
    c = dw.shape[1]
    b = t // seq
    nt = seq // tr
    hb = tr // CONV_HALO
    grp = c // CONV_GROUPS
    avg = jnp.asarray(np.kron(np.eye(CONV_GROUPS), np.full((grp, grp), 1.0 / grp)), BF16)
    cur = lambda off: pl.BlockSpec((tr, c), lambda bi, ti: (bi * nt + ti, col0 + off))
    prev = lambda off: pl.BlockSpec(
        (CONV_HALO, c), lambda bi, ti: (jnp.maximum((bi * nt + ti) * hb - 1, 0), col0 + off))
    vec = pl.BlockSpec((1, c), lambda bi, ti: (0, 0))
    return pl.pallas_call(
        functools.partial(_conv_kernel, tr=tr),
        grid=(b, nt),
        in_specs=[cur(0), cur(1), prev(0), prev(1),
                  pl.BlockSpec((CONV_KERNEL, c), lambda bi, ti: (0, 0)), vec,
                  pl.BlockSpec((c, c), lambda bi, ti: (0, 0)), vec, vec],
        out_specs=pl.BlockSpec((tr, c), lambda bi, ti: (bi * nt + ti, 0)),
        out_shape=jax.ShapeDtypeStruct((t, c), BF16),
        scratch_shapes=[pltpu.VMEM((CONV_HALO + tr, c), F32)],
        compiler_params=_cparams("parallel", "arbitrary"),
        name="conformer_conv",
    )(z, z, z, z, dw, dw_bias.reshape(1, c), avg, gn_gain.reshape(1, c), gn_bias.reshape(1, c))


def _hgrn_level_halves(chunk):
    h = 1
    out = []
    while h < chunk:
        out.append(h)
        h *= 2
    return out


def _hgrn_prefix_matrix(chunk):
    t = np.arange(chunk)[:, None]
    r = np.arange(chunk)[None, :]
    mats = [r <= t, r > t]
    for h in _hgrn_level_halves(chunk):
        mid = (t // (2 * h)) * (2 * h) + h
        right = t >= mid
        mats.append(np.where(right, (r >= mid) & (r <= t), (r > t) & (r < mid)))
    return np.concatenate(mats, axis=0).astype(np.float32)


def _hgrn_kernel(z_ref, pm_ref, loglb_ref, l1mlb_ref, ng_ref, o_ref, st_ref, *, ts, chunk):
    d = o_ref.shape[-1]
    dk = d // HGRN_HEADS

    @pl.when(pl.program_id(1) == 0)
    def _():
        st_ref[...] = jnp.zeros_like(st_ref)

    halves = _hgrn_level_halves(chunk)
    row = lax.broadcasted_iota(jnp.int32, (chunk, 1), 0)
    rr = lax.broadcasted_iota(jnp.int32, (chunk, chunk), 0)
    cc = lax.broadcasted_iota(jnp.int32, (chunk, chunk), 1)
    pm = pm_ref[...]

    for ci in range(ts // chunk):
        r0 = ci * chunk
        f_pre = z_ref[r0:r0 + chunk, d:2 * d].astype(F32)
        log_sig = jnp.minimum(f_pre, 0.0) - jnp.log1p(jnp.exp(-jnp.abs(f_pre)))
        a = loglb_ref[...]
        b = l1mlb_ref[...] + log_sig
        mx = jnp.maximum(a, b)
        log_f = mx + jnp.log1p(jnp.exp(-jnp.abs(a - b)))
        g1, g2, g3 = _split3(log_f)
        ex = _dot(pm, g1) + _dot(pm, g2) + _dot(pm, g3)
        k_all = -jnp.tanh(0.5 * log_f) * (jnp.exp(log_f) + 1.0)
        q_all = _silu(z_ref[r0:r0 + chunk, 0:d].astype(F32))
        last = ex[chunk - 1:chunk, :]

        for h in range(HGRN_HEADS):
            ls = slice(h * dk, (h + 1) * dk)
            q = q_all[:, ls]
            k = k_all[:, ls]
            v = z_ref[r0:r0 + chunk, 2 * d + h * dk:2 * d + (h + 1) * dk]
            scores = jnp.where(rr == cc, _dot_nt(q.astype(BF16), k.astype(BF16)), 0.0)
            for li, half in enumerate(halves):
                w = jnp.exp(ex[(2 + li) * chunk:(3 + li) * chunk, ls])
                right = (row & half) != 0
                qt = jnp.where(right, q * w, 0.0).astype(BF16)
                kt = jnp.where(right, 0.0, k * w).astype(BF16)
                same = (rr ^ cc) < 2 * half
                scores = scores + jnp.where(same, _dot_nt(qt, kt), 0.0)
            st = st_ref[h]
            q_in = (q * jnp.exp(ex[0:chunk, ls])).astype(BF16)
            o = _dot_nt(q_in, st.astype(BF16)) + _dot(scores.astype(BF16), v)
            k_out = (k * jnp.exp(ex[chunk:2 * chunk, ls])).astype(BF16)
            st_ref[h] = st * jnp.exp(last[:, ls]) + _dot_tn(v, k_out)
            gate = _silu(z_ref[r0:r0 + chunk, 3 * d + h * dk:3 * d + (h + 1) * dk].astype(F32))
            o_ref[r0:r0 + chunk, ls] = (_rms(o, ng_ref[:, ls]) * gate).astype(o_ref.dtype)


def hgrn2(z, lower_bound, norm_gain, *, seq, ts, chunk=HGRN_CHUNK):
    t = z.shape[0]
    d = z.shape[1] // 4
    b = t // seq
    nt = seq // ts
    pm = jnp.asarray(_hgrn_prefix_matrix(chunk), BF16)
    lb = lower_bound.astype(F32).reshape(1, d)
    vec = pl.BlockSpec((1, d), lambda bi, ti: (0, 0))
    return pl.pallas_call(
        functools.partial(_hgrn_kernel, ts=ts, chunk=chunk),
        grid=(b, nt),
        in_specs=[pl.BlockSpec((ts, 4 * d), lambda bi, ti: (bi * nt + ti, 0)),
                  pl.BlockSpec(pm.shape, lambda bi, ti: (0, 0)), vec, vec, vec],
        out_specs=pl.BlockSpec((ts, d), lambda bi, ti: (bi * nt + ti, 0)),
        out_shape=jax.ShapeDtypeStruct((t, d), BF16),
        scratch_shapes=[pltpu.VMEM((HGRN_HEADS, d // HGRN_HEADS, d // HGRN_HEADS), F32)],
        compiler_params=_cparams("parallel", "arbitrary"),
        name="hgrn2",
    )(z, pm, jnp.log(lb), jnp.log1p(-lb), norm_gain.reshape(1, d))


def _router_kernel(x_ref, g_ref, r_ref, h_ref, info_ref):
    h = _rms(x_ref[...], g_ref[...])
    h_ref[...] = h
    h_hi, h_lo, _ = _split3(h)
    logits = _dot(h_hi, r_ref[0]) + _dot(h_hi, r_ref[1]) + _dot(h_lo, r_ref[0])
    lane = lax.broadcasted_iota(jnp.int32, logits.shape, 1).astype(F32)
    logits = jnp.where(lane < N_EXPERTS, logits, -jnp.inf)
    m0 = jnp.max(logits, axis=-1, keepdims=True)
    i0 = jnp.min(jnp.where(logits == m0, lane, float(LANES)), axis=-1, keepdims=True)
    rest = jnp.where(lane == i0, -jnp.inf, logits)
    m1 = jnp.max(rest, axis=-1, keepdims=True)
    i1 = jnp.min(jnp.where(rest == m1, lane, float(LANES)), axis=-1, keepdims=True)
    e1 = jnp.exp(m1 - m0)
    g0 = 1.0 / (1.0 + e1)
    g1 = e1 / (1.0 + e1)
    info = jnp.where(lane == 0.0, i0,
                     jnp.where(lane == 1.0, i1,
                               jnp.where(lane == 2.0, g0, jnp.where(lane == 3.0, g1, 0.0))))
    info_ref[...] = info


def router(x, gain, w_router, *, tm):
    t, d = x.shape
    wpad = jnp.zeros((d, LANES), F32).at[:, :N_EXPERTS].set(w_router.astype(F32))
    w2 = jnp.stack(_split3(wpad)[:2])
    return pl.pallas_call(
        _router_kernel,
        grid=(t // tm,),
        in_specs=[pl.BlockSpec((tm, d), lambda i: (i, 0)),
                  pl.BlockSpec((1, d), lambda i: (0, 0)),
                  pl.BlockSpec((2, d, LANES), lambda i: (0, 0, 0))],
        out_specs=[pl.BlockSpec((tm, d), lambda i: (i, 0)),
                   pl.BlockSpec((tm, LANES), lambda i: (i, 0))],
        out_shape=[jax.ShapeDtypeStruct((t, d), F32), jax.ShapeDtypeStruct((t, LANES), F32)],
        compiler_params=_cparams("parallel"),
        name="router",
    )(x, gain.reshape(1, d), w2)


def _row_copy(src_hbm, buf, sem, src_row, dst_row):
    return pltpu.make_async_copy(
        src_hbm.at[pl.ds(pl.multiple_of(src_row * SUBLANES, SUBLANES), SUBLANES), :],
        buf.at[pl.ds(pl.multiple_of(dst_row * SUBLANES, SUBLANES), SUBLANES), :],
        sem)


def _gather_rows(idx_ref, src_hbm, buf, sem, n):
    def start(r, _):
        _row_copy(src_hbm, buf, sem, idx_ref[r], r).start()
        return 0

    def wait(r, _):
        _row_copy(src_hbm, buf, sem, 0, r).wait()
        return 0

    lax.fori_loop(0, n, start, 0)
    lax.fori_loop(0, n, wait, 0)


def _rows_from_tiles(buf, n, chunks):
    return jnp.concatenate(
        [buf[pl.ds(s, n, stride=SUBLANES), :] for s in range(chunks)], axis=-1)


def _dispatch_kernel(idx_ref, h_hbm, o_ref, buf, sem, *, bm):
    _gather_rows(idx_ref.at[0, 0], h_hbm, buf, sem, bm)
    o_ref[...] = _rows_from_tiles(buf, bm, o_ref.shape[1] // LANES).astype(o_ref.dtype)


def dispatch_rows(h, src_rows, *, bm):
    t, d = h.shape
    p = src_rows.shape[0]
    return pl.pallas_call(
        functools.partial(_dispatch_kernel, bm=bm),
        grid=(p // bm,),
        in_specs=[pl.BlockSpec((1, 1, bm), lambda i: (i, 0, 0), memory_space=pltpu.SMEM),
                  pl.BlockSpec(memory_space=pl.ANY)],
        out_specs=pl.BlockSpec((bm, d), lambda i: (i, 0)),
        out_shape=jax.ShapeDtypeStruct((p, d), BF16),
        scratch_shapes=[pltpu.VMEM((bm * SUBLANES, LANES), F32), pltpu.SemaphoreType.DMA],
        compiler_params=_cparams("arbitrary"),
        name="moe_dispatch",
    )(src_rows.reshape(p // bm, 1, bm), h.reshape(t * d // LANES, LANES))


def _combine_kernel(idx_ref, y_hbm, x_ref, g_ref, o_ref, buf, sem, *, tm):
    _gather_rows(idx_ref.at[0, 0], y_hbm, buf, sem, TOP_K * tm)
    chunks = o_ref.shape[1] // LANES
    out = x_ref[...]
    for k in range(TOP_K):
        yk = _rows_from_tiles(buf.at[pl.ds(k * tm * SUBLANES, tm * SUBLANES), :], tm, chunks)
        out = out + g_ref[:, 2 + k:3 + k] * yk
    o_ref[...] = out


def combine_rows(x, y, pos, info, *, tm):
    t, d = x.shape
    p = y.shape[0]
    nt = t // tm
    idx = pos.reshape(TOP_K, nt, tm).transpose(1, 0, 2).reshape(nt, 1, TOP_K * tm)
    return pl.pallas_call(
        functools.partial(_combine_kernel, tm=tm),
        grid=(nt,),
        in_specs=[pl.BlockSpec((1, 1, TOP_K * tm), lambda i: (i, 0, 0), memory_space=pltpu.SMEM),
                  pl.BlockSpec(memory_space=pl.ANY),
                  pl.BlockSpec((tm, d), lambda i: (i, 0)),
                  pl.BlockSpec((tm, LANES), lambda i: (i, 0))],
        out_specs=pl.BlockSpec((tm, d), lambda i: (i, 0)),
        out_shape=jax.ShapeDtypeStruct((t, d), F32),
        scratch_shapes=[pltpu.VMEM((TOP_K * tm * SUBLANES, LANES), F32), pltpu.SemaphoreType.DMA],
        compiler_params=_cparams("arbitrary"),
        name="moe_combine",
    )(idx, y.reshape(p * d // LANES, LANES), x, info)


def _routing_tables(info, *, bm):
    t = info.shape[0]
    experts = info[:, :TOP_K].astype(jnp.int32).T.reshape(-1)
    onehot = (experts[:, None] == jnp.arange(N_EXPERTS)[None, :]).astype(jnp.int32)
    rank = jnp.sum((jnp.cumsum(onehot, axis=0) - onehot) * onehot, axis=1)
    counts = jnp.sum(onehot, axis=0)
    padded = ((counts + bm - 1) // bm) * bm
    ends = jnp.cumsum(padded)
    pos = (ends - padded)[experts] + rank
    p = TOP_K * t + N_EXPERTS * bm
    tokens = jnp.tile(jnp.arange(t, dtype=jnp.int32), TOP_K)
    src_rows = jnp.zeros((p,), jnp.int32).at[pos].set(tokens)
    block_expert = jnp.searchsorted(ends, jnp.arange(p // bm, dtype=jnp.int32) * bm, side="right")
    block_expert = jnp.minimum(block_expert, N_EXPERTS - 1).astype(jnp.int32)
    return src_rows, pos.reshape(TOP_K, t).astype(jnp.int32), block_expert


def moe_layer(x, gain, w_router, w_in, w_out, *, tm_route, bm, fc, tm_combine):
    h, info = router(x, gain, w_router, tm=tm_route)
    src_rows, pos, block_expert = _routing_tables(info, bm=bm)
    xs = dispatch_rows(h, src_rows, bm=bm)
    ys = swiglu(xs, gain, w_in, w_out, block_expert, tm=bm, fc=fc, dense=False)
    return combine_rows(x, ys, pos, info, tm=tm_combine)


def kernel(x, mix_norm, ffn_norm, ab_w_in, ab_w_out, da_q_gain, da_k_gain, da_lambda, da_subln,
           conv_dw, conv_dw_bias, conv_gn_gain, conv_gn_bias, hgrn_w_in, hgrn_lb_logits, hgrn_norm,
           hgrn_w_out, ffn_w_in, ffn_w_out, moe_router, moe_w_in, moe_w_out):
    batch, seq, d = x.shape
    depth = mix_norm.shape[0]
    attn_w = DA_HEADS * LANES
    xf = x.reshape(batch * seq, d)

    lb_all = jnp.cumsum(jax.nn.softmax(hgrn_lb_logits.astype(F32), axis=0), axis=0)
    lb_all = lb_all - lb_all[0:1]
    tm_ffn = 512
    dense_expert = jnp.zeros((xf.shape[0] // tm_ffn,), jnp.int32)

    for layer in range(depth):
        j = layer // 2
        if layer % 2 == 0:
            z = norm_matmul(xf, mix_norm[layer], ab_w_in[j].astype(BF16), tm=1024, tn=512)
            lam_init = 0.8 - 0.6 * math.exp(-0.3 * layer)
            lv = da_lambda[j].astype(F32)
            lam = jnp.exp(jnp.sum(lv[0] * lv[1])) - jnp.exp(jnp.sum(lv[2] * lv[3])) + lam_init
            a = diff_attention(z, lam, da_q_gain[j], da_k_gain[j], da_subln[j],
                               seq=seq, tq=256, lam_init=lam_init)
            c = conformer_conv(z, conv_dw[j], conv_dw_bias[j], conv_gn_gain[j], conv_gn_bias[j],
                               seq=seq, tr=512, col0=3)
            w_o = ab_w_out[j].astype(BF16)
            xf = proj_residual([a, c], [w_o[:attn_w], w_o[attn_w:]], xf, tm=1024)
            xf = swiglu(xf, ffn_norm[layer], ffn_w_in[j].astype(BF16)[None],
                        ffn_w_out[j].astype(BF16)[None], dense_expert, tm=tm_ffn, fc=1408, dense=True)
        else:
            z = norm_matmul(xf, mix_norm[layer], hgrn_w_in[j].astype(BF16), tm=1024, tn=512)
            o = hgrn2(z, lb_all[j], hgrn_norm[j], seq=seq, ts=256)
            xf = proj_residual([o], [hgrn_w_out[j].astype(BF16)], xf, tm=1024)
            xf = moe_layer(xf, ffn_norm[layer], moe_router[j], moe_w_in[j].astype(BF16),
                           moe_w_out[j].astype(BF16), tm_route=1024, bm=1024, fc=512,
                           tm_combine=256)
    return xf.reshape(batch, seq, d)
```

```python
import functools
import math

import numpy as np
import jax
import jax.numpy as jnp
from jax import lax
from jax.experimental import pallas as pl
from jax.experimental.pallas import tpu as pltpu

F32 = jnp.float32
BF16 = jnp.bfloat16

EPS = 1e-6
LANES = 128
SUBLANES = 8
VMEM_LIMIT = 48 * 1024 * 1024

DA_HEADS = 4
DA_HEAD_DIM = 64
CONV_KERNEL = 31
CONV_HALO = 32
CONV_GROUPS = 8
HGRN_HEADS = 8
HGRN_CHUNK = 128
N_EXPERTS = 8
TOP_K = 2


def _cparams(*sem):
    return pltpu.CompilerParams(dimension_semantics=sem, vmem_limit_bytes=VMEM_LIMIT)


def _rms(x, gain):
    ms = jnp.mean(x * x, axis=-1, keepdims=True)
    return x * lax.rsqrt(ms + EPS) * gain


def _silu(x):
    return x * jax.nn.sigmoid(x)


def _split3(x):
    a = x.astype(BF16)
    r = x - a.astype(F32)
    b = r.astype(BF16)
    c = (r - b.astype(F32)).astype(BF16)
    return a, b, c


def _dot(a, b):
    return jnp.dot(a, b, preferred_element_type=F32)


def _dot_nt(a, b):
    return lax.dot_general(a, b, (((1,), (1,)), ((), ())), preferred_element_type=F32)


def _dot_tn(a, b):
    return lax.dot_general(a, b, (((0,), (0,)), ((), ())), preferred_element_type=F32)


def _norm_matmul_kernel(x_ref, g_ref, w_ref, o_ref, h_ref):
    @pl.when(pl.program_id(1) == 0)
    def _():
        h_ref[...] = _rms(x_ref[...], g_ref[...]).astype(BF16)

    o_ref[...] = _dot(h_ref[...], w_ref[...]).astype(o_ref.dtype)


def norm_matmul(x, gain, w, *, tm, tn):
    t, d = x.shape
    n = w.shape[1]
    return pl.pallas_call(
        _norm_matmul_kernel,
        grid=(t // tm, n // tn),
        in_specs=[
            pl.BlockSpec((tm, d), lambda i, j: (i, 0)),
            pl.BlockSpec((1, d), lambda i, j: (0, 0)),
            pl.BlockSpec((d, tn), lambda i, j: (0, j)),
        ],
        out_specs=pl.BlockSpec((tm, tn), lambda i, j: (i, j)),
        out_shape=jax.ShapeDtypeStruct((t, n), BF16),
        scratch_shapes=[pltpu.VMEM((tm, d), BF16)],
        compiler_params=_cparams("parallel", "arbitrary"),
        name="norm_matmul",
    )(x, gain.reshape(1, d), w)


def _proj_residual_kernel(*refs):
    n_in = (len(refs) - 2) // 2
    res_ref, o_ref = refs[2 * n_in], refs[2 * n_in + 1]
    acc = res_ref[...]
    for i in range(n_in):
        acc = acc + _dot(refs[i][...], refs[n_in + i][...])
    o_ref[...] = acc


def proj_residual(acts, weights, res, *, tm):
    t, d = res.shape
    in_specs = [pl.BlockSpec((tm, a.shape[1]), lambda i: (i, 0)) for a in acts]
    in_specs += [pl.BlockSpec(w.shape, lambda i: (0, 0)) for w in weights]
    in_specs += [pl.BlockSpec((tm, d), lambda i: (i, 0))]
    return pl.pallas_call(
        _proj_residual_kernel,
        grid=(t // tm,),
        in_specs=in_specs,
        out_specs=pl.BlockSpec((tm, d), lambda i: (i, 0)),
        out_shape=jax.ShapeDtypeStruct((t, d), F32),
        compiler_params=_cparams("parallel"),
        name="proj_residual",
    )(*acts, *weights, res)


def _swiglu_kernel(x_ref, g_ref, wg_ref, wu_ref, wo_ref, o_ref, h_ref, acc_ref):
    c = pl.program_id(1)

    @pl.when(c == 0)
    def _():
        h_ref[...] = _rms(x_ref[...], g_ref[...]).astype(BF16)
        acc_ref[...] = jnp.zeros_like(acc_ref)

    h = h_ref[...]
    act = _silu(_dot(h, wg_ref[...])) * _dot(h, wu_ref[...])
    acc_ref[...] += _dot(act.astype(BF16), wo_ref[...])

    @pl.when(c == pl.num_programs(1) - 1)
    def _():
        o_ref[...] = x_ref[...] + acc_ref[...]


def swiglu(x, gain, w_in, w_out, *, tm, fc):
    t, d = x.shape
    f = w_out.shape[0]
    nf = f // fc
    return pl.pallas_call(
        _swiglu_kernel,
        grid=(t // tm, nf),
        in_specs=[
            pl.BlockSpec((tm, d), lambda i, c: (i, 0)),
            pl.BlockSpec((1, d), lambda i, c: (0, 0)),
            pl.BlockSpec((d, fc), lambda i, c: (0, c)),
            pl.BlockSpec((d, fc), lambda i, c: (0, c + nf)),
            pl.BlockSpec((fc, d), lambda i, c: (c, 0)),
        ],
        out_specs=pl.BlockSpec((tm, d), lambda i, c: (i, 0)),
        out_shape=jax.ShapeDtypeStruct((t, d), F32),
        scratch_shapes=[pltpu.VMEM((tm, d), BF16), pltpu.VMEM((tm, d), F32)],
        compiler_params=_cparams("parallel", "arbitrary"),
        name="swiglu_dense",
    )(x, gain.reshape(1, d), w_in, w_in, w_out)


def _half_rms(x, gain, low):
    sq = x * x
    s_lo = jnp.sum(jnp.where(low, sq, 0.0), axis=-1, keepdims=True)
    s_hi = jnp.sum(jnp.where(low, 0.0, sq), axis=-1, keepdims=True)
    ms = jnp.where(low, s_lo, s_hi) * (1.0 / DA_HEAD_DIM)
    return x * lax.rsqrt(ms + EPS) * gain


def _diff_attn_kernel(lam_ref, q_ref, k_ref, v_ref, qg_ref, kg_ref, sg_ref, o_ref,
                      kn_ref, vt_ref, s_ref, *, tq, lam_init):
    seq = q_ref.shape[0]
    low = lax.broadcasted_iota(jnp.int32, (1, LANES), 1) < DA_HEAD_DIM
    kn_ref[...] = _half_rms(k_ref[...].astype(F32), kg_ref[...], low).astype(BF16)
    vt_ref[...] = v_ref[...].astype(F32).T.astype(BF16)

    scale = DA_HEAD_DIM ** -0.5 * math.log2(math.e)
    key = lax.broadcasted_iota(jnp.int32, (tq, tq), 0)
    qry = lax.broadcasted_iota(jnp.int32, (tq, tq), 1)
    lam = lam_ref[0]

    for qi in range(seq // tq):
        rows = slice(qi * tq, (qi + 1) * tq)
        qn = _half_rms(q_ref[rows, :].astype(F32), qg_ref[...], low) * scale
        q_maps = (jnp.where(low, qn, 0.0).astype(BF16), jnp.where(low, 0.0, qn).astype(BF16))
        outs = []
        for c in range(2):
            m = jnp.full((1, tq), -jnp.inf, F32)
            for j in range(qi + 1):
                s = _dot_nt(kn_ref[j * tq:(j + 1) * tq, :], q_maps[c])
                if j == qi:
                    s = jnp.where(key <= qry, s, -jnp.inf)
                s_ref[c, j] = s
                m = jnp.maximum(m, jnp.max(s, axis=0, keepdims=True))
            l = jnp.zeros((1, tq), F32)
            acc = jnp.zeros((LANES, tq), F32)
            for j in range(qi + 1):
                p = jnp.exp2(s_ref[c, j] - m)
                l = l + jnp.sum(p, axis=0, keepdims=True)
                acc = acc + _dot(vt_ref[:, j * tq:(j + 1) * tq], p.astype(BF16))
            outs.append(acc / l)
        o = outs[0] - lam * outs[1]
        ms = jnp.mean(o * o, axis=0, keepdims=True)
        o = o * lax.rsqrt(ms + EPS) * sg_ref[...] * (1.0 - lam_init)
        o_ref[rows, :] = o.T.astype(o_ref.dtype)


def diff_attention(z, lam, q_gain, k_gain, subln_gain, *, seq, tq, lam_init):
    t = z.shape[0]
    b = t // seq
    nq = seq // tq
    qg = jnp.concatenate([q_gain, q_gain]).reshape(1, LANES)
    kg = jnp.concatenate([k_gain, k_gain]).reshape(1, LANES)
    sg = subln_gain.astype(F32).reshape(LANES, 1)
    vec = pl.BlockSpec((1, LANES), lambda bi, h: (0, 0))
    head = lambda off: pl.BlockSpec((seq, LANES), lambda bi, h: (bi, off + h))
    return pl.pallas_call(
        functools.partial(_diff_attn_kernel, tq=tq, lam_init=lam_init),
        grid=(b, DA_HEADS),
        in_specs=[pl.BlockSpec(memory_space=pltpu.SMEM),
                  head(0), head(DA_HEADS), head(2 * DA_HEADS),
                  vec, vec, pl.BlockSpec((LANES, 1), lambda bi, h: (0, 0))],
        out_specs=pl.BlockSpec((seq, LANES), lambda bi, h: (bi, h)),
        out_shape=jax.ShapeDtypeStruct((t, DA_HEADS * LANES), BF16),
        scratch_shapes=[pltpu.VMEM((seq, LANES), BF16),
                        pltpu.VMEM((LANES, seq), BF16),
                        pltpu.VMEM((2, nq, tq, tq), F32)],
        compiler_params=_cparams("parallel", "parallel"),
        name="diff_attention",
    )(lam.reshape(1), z, z, z, qg, kg, sg)


def _conv_kernel(val_ref, gate_ref, pval_ref, pgate_ref, dw_ref, b_ref, avg_ref, gg_ref, gb_ref,
                 o_ref, g_ref, *, tr):
    ti = pl.program_id(1)
    halo = pval_ref[...].astype(F32) * jax.nn.sigmoid(pgate_ref[...].astype(F32))
    g_ref[0:CONV_HALO, :] = jnp.where(ti == 0, 0.0, halo)
    g_ref[CONV_HALO:, :] = val_ref[...].astype(F32) * jax.nn.sigmoid(gate_ref[...].astype(F32))

    first = CONV_HALO - (CONV_KERNEL - 1)
    y = jnp.broadcast_to(b_ref[...], o_ref.shape)
    for k in range(CONV_KERNEL):
        y = y + g_ref[first + k:first + k + tr, :] * dw_ref[k:k + 1, :]

    avg = avg_ref[...]

    def group_mean(u):
        a, b, c = _split3(u)
        return _dot(a, avg) + _dot(b, avg) + _dot(c, avg)

    d = y - group_mean(y)
    var = group_mean(d * d)
    yn = d * lax.rsqrt(var + EPS) * gg_ref[...] + gb_ref[...]
    o_ref[...] = _silu(yn).astype(o_ref.dtype)


def conformer_conv(z, dw, dw_bias, gn_gain, gn_bias, *, seq, tr, col0):
    t = z.shape[0]
    c = dw.shape[1]
    b = t // seq
    nt = seq // tr
    hb = tr // CONV_HALO
    grp = c // CONV_GROUPS
    avg = jnp.asarray(np.kron(np.eye(CONV_GROUPS), np.full((grp, grp), 1.0 / grp)), BF16)
    cur = lambda off: pl.BlockSpec((tr, c), lambda bi, ti: (bi * nt + ti, col0 + off))
    prev = lambda off: pl.BlockSpec(
        (CONV_HALO, c), lambda bi, ti: (jnp.maximum((bi * nt + ti) * hb - 1, 0), col0 + off))
    vec = pl.BlockSpec((1, c), lambda bi, ti: (0, 0))
    return pl.pallas_call(
        functools.partial(_conv_kernel, tr=tr),
        grid=(b, nt),
        in_specs=[cur(0), cur(1), prev(0), prev(1),
                  pl.BlockSpec((CONV_KERNEL, c), lambda bi, ti: (0, 0)), vec,
                  pl.BlockSpec((c, c), lambda bi, ti: (0, 0)), vec, vec],
        out_specs=pl.BlockSpec((tr, c), lambda bi, ti: (bi * nt + ti, 0)),
        out_shape=jax.ShapeDtypeStruct((t, c), BF16),
        scratch_shapes=[pltpu.VMEM((CONV_HALO + tr, c), F32)],
        compiler_params=_cparams("parallel", "arbitrary"),
        name="conformer_conv",
    )(z, z, z, z, dw, dw_bias.reshape(1, c), avg, gn_gain.reshape(1, c), gn_bias.reshape(1, c))


def _hgrn_level_halves(chunk):
    h = 1
    out = []
    while h < chunk:
        out.append(h)
        h *= 2
    return out


def _hgrn_prefix_matrix(chunk):
    t = np.arange(chunk)[:, None]
    r = np.arange(chunk)[None, :]
    mats = [r <= t, r > t]
    for h in _hgrn_level_halves(chunk):
        mid = (t // (2 * h)) * (2 * h) + h
        right = t >= mid
        mats.append(np.where(right, (r >= mid) & (r <= t), (r > t) & (r < mid)))
    return np.concatenate(mats, axis=0).astype(np.float32)


def _hgrn_kernel(z_ref, pm_ref, loglb_ref, l1mlb_ref, ng_ref, o_ref, st_ref, *, ts, chunk):
    d = o_ref.shape[-1]
    dk = d // HGRN_HEADS

    @pl.when(pl.program_id(1) == 0)
    def _():
        st_ref[...] = jnp.zeros_like(st_ref)

    halves = _hgrn_level_halves(chunk)
    row = lax.broadcasted_iota(jnp.int32, (chunk, 1), 0)
    rr = lax.broadcasted_iota(jnp.int32, (chunk, chunk), 0)
    cc = lax.broadcasted_iota(jnp.int32, (chunk, chunk), 1)
    pm = pm_ref[...]

    for ci in range(ts // chunk):
        r0 = ci * chunk
        f_pre = z_ref[r0:r0 + chunk, d:2 * d].astype(F32)
        log_sig = jnp.minimum(f_pre, 0.0) - jnp.log1p(jnp.exp(-jnp.abs(f_pre)))
        a = loglb_ref[...]
        b = l1mlb_ref[...] + log_sig
        mx = jnp.maximum(a, b)
        log_f = mx + jnp.log1p(jnp.exp(-jnp.abs(a - b)))
        g1, g2, g3 = _split3(log_f)
        ex = _dot(pm, g1) + _dot(pm, g2) + _dot(pm, g3)
        k_all = -jnp.tanh(0.5 * log_f) * (jnp.exp(log_f) + 1.0)
        q_all = _silu(z_ref[r0:r0 + chunk, 0:d].astype(F32))
        last = ex[chunk - 1:chunk, :]

        for h in range(HGRN_HEADS):
            ls = slice(h * dk, (h + 1) * dk)
            q = q_all[:, ls]
            k = k_all[:, ls]
            v = z_ref[r0:r0 + chunk, 2 * d + h * dk:2 * d + (h + 1) * dk]
            scores = jnp.where(rr == cc, _dot_nt(q.astype(BF16), k.astype(BF16)), 0.0)
            for li, half in enumerate(halves):
                w = jnp.exp(ex[(2 + li) * chunk:(3 + li) * chunk, ls])
                right = (row & half) != 0
                qt = jnp.where(right, q * w, 0.0).astype(BF16)
                kt = jnp.where(right, 0.0, k * w).astype(BF16)
                same = (rr ^ cc) < 2 * half
                scores = scores + jnp.where(same, _dot_nt(qt, kt), 0.0)
            st = st_ref[h]
            q_in = (q * jnp.exp(ex[0:chunk, ls])).astype(BF16)
            o = _dot_nt(q_in, st.astype(BF16)) + _dot(scores.astype(BF16), v)
            k_out = (k * jnp.exp(ex[chunk:2 * chunk, ls])).astype(BF16)
            st_ref[h] = st * jnp.exp(last[:, ls]) + _dot_tn(v, k_out)
            gate = _silu(z_ref[r0:r0 + chunk, 3 * d + h * dk:3 * d + (h + 1) * dk].astype(F32))
            o_ref[r0:r0 + chunk, ls] = (_rms(o, ng_ref[:, ls]) * gate).astype(o_ref.dtype)


def hgrn2(z, lower_bound, norm_gain, *, seq, ts, chunk=HGRN_CHUNK):
    t = z.shape[0]
    d = z.shape[1] // 4
    b = t // seq
    nt = seq // ts
    pm = jnp.asarray(_hgrn_prefix_matrix(chunk), BF16)
    lb = lower_bound.astype(F32).reshape(1, d)
    vec = pl.BlockSpec((1, d), lambda bi, ti: (0, 0))
    return pl.pallas_call(
        functools.partial(_hgrn_kernel, ts=ts, chunk=chunk),
        grid=(b, nt),
        in_specs=[pl.BlockSpec((ts, 4 * d), lambda bi, ti: (bi * nt + ti, 0)),
                  pl.BlockSpec(pm.shape, lambda bi, ti: (0, 0)), vec, vec, vec],
        out_specs=pl.BlockSpec((ts, d), lambda bi, ti: (bi * nt + ti, 0)),
        out_shape=jax.ShapeDtypeStruct((t, d), BF16),
        scratch_shapes=[pltpu.VMEM((HGRN_HEADS, d // HGRN_HEADS, d // HGRN_HEADS), F32)],
        compiler_params=_cparams("parallel", "arbitrary"),
        name="hgrn2",
    )(z, pm, jnp.log(lb), jnp.log1p(-lb), norm_gain.reshape(1, d))


def _rows_from_tiles(buf, n, chunks):
    return jnp.concatenate(
        [buf[pl.ds(s, n, stride=SUBLANES), :] for s in range(chunks)], axis=-1)


def _rows_to_tiles(buf, val):
    n = val.shape[0]
    for s in range(val.shape[1] // LANES):
        buf[pl.ds(s, n, stride=SUBLANES), :] = val[:, s * LANES:(s + 1) * LANES]


def _router_kernel(x_ref, g_ref, r_ref, h_ref, info_ref):
    h = _rms(x_ref[...], g_ref[...])
    _rows_to_tiles(h_ref, h)
    h_hi, h_lo, _ = _split3(h)
    logits = _dot(h_hi, r_ref[0]) + _dot(h_hi, r_ref[1]) + _dot(h_lo, r_ref[0])
    lane = lax.broadcasted_iota(jnp.int32, logits.shape, 1).astype(F32)
    logits = jnp.where(lane < N_EXPERTS, logits, -jnp.inf)
    m0 = jnp.max(logits, axis=-1, keepdims=True)
    i0 = jnp.min(jnp.where(logits == m0, lane, float(LANES)), axis=-1, keepdims=True)
    rest = jnp.where(lane == i0, -jnp.inf, logits)
    m1 = jnp.max(rest, axis=-1, keepdims=True)
    i1 = jnp.min(jnp.where(rest == m1, lane, float(LANES)), axis=-1, keepdims=True)
    e1 = jnp.exp(m1 - m0)
    g0 = 1.0 / (1.0 + e1)
    g1 = e1 / (1.0 + e1)
    info = jnp.where(lane == 0.0, i0,
                     jnp.where(lane == 1.0, i1,
                               jnp.where(lane == 2.0, g0, jnp.where(lane == 3.0, g1, 0.0))))
    info_ref[...] = info


def router(x, gain, w_router, *, tm):
    t, d = x.shape
    wpad = jnp.zeros((d, LANES), F32).at[:, :N_EXPERTS].set(w_router.astype(F32))
    w2 = jnp.stack(_split3(wpad)[:2])
    return pl.pallas_call(
        _router_kernel,
        grid=(t // tm,),
        in_specs=[pl.BlockSpec((tm, d), lambda i: (i, 0)),
                  pl.BlockSpec((1, d), lambda i: (0, 0)),
                  pl.BlockSpec((2, d, LANES), lambda i: (0, 0, 0))],
        out_specs=[pl.BlockSpec((tm * SUBLANES, LANES), lambda i: (i, 0)),
                   pl.BlockSpec((tm, LANES), lambda i: (i, 0))],
        out_shape=[jax.ShapeDtypeStruct((t * SUBLANES, LANES), F32),
                   jax.ShapeDtypeStruct((t, LANES), F32)],
        compiler_params=_cparams("parallel"),
        name="router",
    )(x, gain.reshape(1, d), w2)


def _experts_kernel(eid_ref, nused_ref, cur_ref, nxt_ref, h_hbm, wg_ref, wu_ref, wo_ref, y_hbm,
                    xbuf, ybuf, h_ref, acc_ref, sem_x, sem_y, *, bm, gather_steps):
    del eid_ref
    b = pl.program_id(0)
    c = pl.program_id(1)
    last_c = pl.num_programs(1) - 1
    nused = nused_ref[0]
    slot = lax.rem(b, 2)
    per_step = bm // gather_steps
    chunks = h_ref.shape[1] // LANES

    def tile(ref, row):
        return ref.at[pl.ds(pl.multiple_of(row * SUBLANES, SUBLANES), SUBLANES), :]

    def start_gather(idx_ref, dst_slot, first, count):
        def body(i, carry):
            r = first + i
            pltpu.make_async_copy(tile(h_hbm, idx_ref[0, 0, r]), tile(xbuf.at[dst_slot], r),
                                  sem_x.at[dst_slot]).start()
            return carry
        lax.fori_loop(0, count, body, 0, unroll=8)

    def wait_gather(dst_slot):
        pltpu.make_async_copy(h_hbm.at[pl.ds(0, bm * SUBLANES), :], xbuf.at[dst_slot],
                              sem_x.at[dst_slot]).wait()

    def start_scatter():
        def body(r, carry):
            pltpu.make_async_copy(tile(ybuf, r), tile(y_hbm, cur_ref[0, 0, bm + r]), sem_y).start()
            return carry
        lax.fori_loop(0, bm, body, 0, unroll=8)

    def wait_scatter():
        pltpu.make_async_copy(ybuf, y_hbm.at[pl.ds(0, bm * SUBLANES), :], sem_y).wait()

    @pl.when(b < nused)
    def _():
        @pl.when(c == 0)
        def _():
            @pl.when(b == 0)
            def _():
                start_gather(cur_ref, 0, 0, bm)
            wait_gather(slot)
            h_ref[...] = _rows_from_tiles(xbuf.at[slot], bm, chunks).astype(BF16)
            acc_ref[...] = jnp.zeros_like(acc_ref)

        @pl.when((c < gather_steps) & (b + 1 < nused))
        def _():
            start_gather(nxt_ref, 1 - slot, c * per_step, per_step)

        h = h_ref[...]
        act = _silu(_dot(h, wg_ref[...])) * _dot(h, wu_ref[...])
        acc_ref[...] += _dot(act.astype(BF16), wo_ref[...])

        @pl.when(c == last_c)
        def _():
            @pl.when(b > 0)
            def _():
                wait_scatter()
            _rows_to_tiles(ybuf, acc_ref[...])
            start_scatter()

            @pl.when(b == nused - 1)
            def _():
                wait_scatter()

    @pl.when((b >= nused) & (c == last_c))
    def _():
        start_scatter()
        wait_scatter()


def expert_swiglu(h_tiles, table, block_expert, nused, w_in, w_out, *, bm, fc, out_rows):
    nblk = table.shape[0]
    d = w_in.shape[1]
    f = w_out.shape[1]
    nf = f // fc
    gather_steps = min(4, nf)
    assert d == SUBLANES * LANES and bm % gather_steps == 0 and f % fc == 0

    def chunk(b, c, nu):
        return jnp.where(b < nu[0], c, nf - 1)

    grid_spec = pltpu.PrefetchScalarGridSpec(
        num_scalar_prefetch=2,
        grid=(nblk, nf),
        in_specs=[
            pl.BlockSpec((1, 1, 2 * bm), lambda b, c, eid, nu: (b, 0, 0), memory_space=pltpu.SMEM),
            pl.BlockSpec((1, 1, 2 * bm), lambda b, c, eid, nu: (jnp.minimum(b + 1, nblk - 1), 0, 0),
                         memory_space=pltpu.SMEM),
            pl.BlockSpec(memory_space=pl.ANY),
            pl.BlockSpec((None, d, fc), lambda b, c, eid, nu: (eid[b], 0, chunk(b, c, nu))),
            pl.BlockSpec((None, d, fc), lambda b, c, eid, nu: (eid[b], 0, chunk(b, c, nu) + nf)),
            pl.BlockSpec((None, fc, d), lambda b, c, eid, nu: (eid[b], chunk(b, c, nu), 0)),
        ],
        out_specs=pl.BlockSpec(memory_space=pl.ANY),
        scratch_shapes=[pltpu.VMEM((2, bm * SUBLANES, LANES), F32),
                        pltpu.VMEM((bm * SUBLANES, LANES), F32),
                        pltpu.VMEM((bm, d), BF16),
                        pltpu.VMEM((bm, d), F32),
                        pltpu.SemaphoreType.DMA((2,)),
                        pltpu.SemaphoreType.DMA],
    )
    return pl.pallas_call(
        functools.partial(_experts_kernel, bm=bm, gather_steps=gather_steps),
        grid_spec=grid_spec,
        out_shape=jax.ShapeDtypeStruct((out_rows * SUBLANES, LANES), F32),
        compiler_params=_cparams("arbitrary", "arbitrary"),
        name="expert_swiglu",
    )(block_expert, nused, table, table, h_tiles, w_in, w_in, w_out)


def _combine_kernel(x_ref, y0_ref, y1_ref, g_ref, o_ref):
    tm = x_ref.shape[0]
    chunks = x_ref.shape[1] // LANES
    out = x_ref[...]
    for k, y_ref in enumerate((y0_ref, y1_ref)):
        out = out + g_ref[:, TOP_K + k:TOP_K + k + 1] * _rows_from_tiles(y_ref, tm, chunks)
    o_ref[...] = out


def combine(x, y_tiles, info, *, tm):
    t, d = x.shape
    nt = t // tm
    return pl.pallas_call(
        _combine_kernel,
        grid=(nt,),
        in_specs=[pl.BlockSpec((tm, d), lambda i: (i, 0)),
                  pl.BlockSpec((tm * SUBLANES, LANES), lambda i: (i, 0)),
                  pl.BlockSpec((tm * SUBLANES, LANES), lambda i: (i + nt, 0)),
                  pl.BlockSpec((tm, LANES), lambda i: (i, 0))],
        out_specs=pl.BlockSpec((tm, d), lambda i: (i, 0)),
        out_shape=jax.ShapeDtypeStruct((t, d), F32),
        compiler_params=_cparams("parallel"),
        name="moe_combine",
    )(x, y_tiles, y_tiles, info)


def _routing_tables(info, *, bm):
    t = info.shape[0]
    n_pairs = TOP_K * t
    experts = info[:, :TOP_K].astype(jnp.int32).T.reshape(-1)
    onehot = (experts[:, None] == jnp.arange(N_EXPERTS)[None, :]).astype(jnp.int32)
    rank = jnp.sum((jnp.cumsum(onehot, axis=0) - onehot) * onehot, axis=1)
    counts = jnp.sum(onehot, axis=0)
    padded = ((counts + bm - 1) // bm) * bm
    ends = jnp.cumsum(padded)
    slot = (ends - padded)[experts] + rank
    n_slots = n_pairs + N_EXPERTS * bm
    nblk = n_slots // bm
    pair = jnp.full((n_slots,), -1, jnp.int32).at[slot].set(jnp.arange(n_pairs, dtype=jnp.int32))
    is_pad = pair < 0
    token_row = jnp.where(is_pad, 0, pair % t)
    pad_row = n_pairs + jnp.cumsum(is_pad.astype(jnp.int32)) - 1
    out_row = jnp.where(is_pad, pad_row, pair)
    table = jnp.concatenate([token_row.reshape(nblk, 1, bm), out_row.reshape(nblk, 1, bm)], axis=2)
    nused = (ends[-1] // bm).astype(jnp.int32)
    starts = jnp.arange(nblk, dtype=jnp.int32) * bm
    block_expert = jnp.sum((starts[:, None] >= ends[None, :]).astype(jnp.int32), axis=1)
    block_expert = jnp.minimum(block_expert, N_EXPERTS - 1)
    last_used = jnp.sum(jnp.where(jnp.arange(nblk) == nused - 1, block_expert, 0))
    block_expert = jnp.where(jnp.arange(nblk) < nused, block_expert, last_used).astype(jnp.int32)
    return table, block_expert, nused.reshape(1), n_slots


def moe_layer(x, gain, w_router, w_in, w_out, *, tm_route, bm, fc, tm_combine):
    h_tiles, info = router(x, gain, w_router, tm=tm_route)
    table, block_expert, nused, n_slots = _routing_tables(info, bm=bm)
    y_tiles = expert_swiglu(h_tiles, table, block_expert, nused, w_in, w_out,
                            bm=bm, fc=fc, out_rows=n_slots)
    return combine(x, y_tiles, info, tm=tm_combine)


def kernel(x, mix_norm, ffn_norm, ab_w_in, ab_w_out, da_q_gain, da_k_gain, da_lambda, da_subln,
           conv_dw, conv_dw_bias, conv_gn_gain, conv_gn_bias, hgrn_w_in, hgrn_lb_logits, hgrn_norm,
           hgrn_w_out, ffn_w_in, ffn_w_out, moe_router, moe_w_in, moe_w_out):
    batch, seq, d = x.shape
    depth = mix_norm.shape[0]
    attn_w = DA_HEADS * LANES
    xf = x.reshape(batch * seq, d)

    lb_all = jnp.cumsum(jax.nn.softmax(hgrn_lb_logits.astype(F32), axis=0), axis=0)
    lb_all = lb_all - lb_all[0:1]
    for layer in range(depth):
        j = layer // 2
        if layer % 2 == 0:
            z = norm_matmul(xf, mix_norm[layer], ab_w_in[j].astype(BF16), tm=1024, tn=512)
            lam_init = 0.8 - 0.6 * math.exp(-0.3 * layer)
            lv = da_lambda[j].astype(F32)
            lam = jnp.exp(jnp.sum(lv[0] * lv[1])) - jnp.exp(jnp.sum(lv[2] * lv[3])) + lam_init
            a = diff_attention(z, lam, da_q_gain[j], da_k_gain[j], da_subln[j],
                               seq=seq, tq=256, lam_init=lam_init)
            c = conformer_conv(z, conv_dw[j], conv_dw_bias[j], conv_gn_gain[j], conv_gn_bias[j],
                               seq=seq, tr=512, col0=3)
            w_o = ab_w_out[j].astype(BF16)
            xf = proj_residual([a, c], [w_o[:attn_w], w_o[attn_w:]], xf, tm=1024)
            xf = swiglu(xf, ffn_norm[layer], ffn_w_in[j].astype(BF16), ffn_w_out[j].astype(BF16),
                        tm=512, fc=1408)
        else:
            z = norm_matmul(xf, mix_norm[layer], hgrn_w_in[j].astype(BF16), tm=1024, tn=512)
            o = hgrn2(z, lb_all[j], hgrn_norm[j], seq=seq, ts=256)
            xf = proj_residual([o], [hgrn_w_out[j].astype(BF16)], xf, tm=1024)
            xf = moe_layer(xf, ffn_norm[layer], moe_router[j], moe_w_in[j].astype(BF16),
                           moe_w_out[j].astype(BF16), tm_route=1024, bm=1024, fc=512,
                           tm_combine=256)
    return xf.reshape(batch, seq, d)
```

```python
import functools
import math

import numpy as np
import jax
import jax.numpy as jnp
from jax import lax
from jax.experimental import pallas as pl
from jax.experimental.pallas import tpu as pltpu

F32 = jnp.float32
BF16 = jnp.bfloat16

EPS = 1e-6
LANES = 128
SUBLANES = 8
VMEM_LIMIT = 48 * 1024 * 1024

DA_HEADS = 4
DA_HEAD_DIM = 64
CONV_KERNEL = 31
CONV_HALO = 32
CONV_GROUPS = 8
HGRN_HEADS = 8
HGRN_CHUNK = 128
N_EXPERTS = 8
TOP_K = 2


def _cparams(*sem):
    return pltpu.CompilerParams(dimension_semantics=sem, vmem_limit_bytes=VMEM_LIMIT)


def _rms(x, gain):
    ms = jnp.mean(x * x, axis=-1, keepdims=True)
    return x * lax.rsqrt(ms + EPS) * gain


def _silu(x):
    return x * jax.nn.sigmoid(x)


def _split3(x):
    a = x.astype(BF16)
    r = x - a.astype(F32)
    b = r.astype(BF16)
    c = (r - b.astype(F32)).astype(BF16)
    return a, b, c


def _dot(a, b):
    return jnp.dot(a, b, preferred_element_type=F32)


def _dot_nt(a, b):
    return lax.dot_general(a, b, (((1,), (1,)), ((), ())), preferred_element_type=F32)


def _dot_tn(a, b):
    return lax.dot_general(a, b, (((0,), (0,)), ((), ())), preferred_element_type=F32)


def _norm_matmul_kernel(x_ref, g_ref, w_ref, o_ref, h_ref):
    @pl.when(pl.program_id(1) == 0)
    def _():
        h_ref[...] = _rms(x_ref[...], g_ref[...]).astype(BF16)

    o_ref[...] = _dot(h_ref[...], w_ref[...]).astype(o_ref.dtype)


def norm_matmul(x, gain, w, *, tm, tn):
    t, d = x.shape
    n = w.shape[1]
    return pl.pallas_call(
        _norm_matmul_kernel,
        grid=(t // tm, n // tn),
        in_specs=[
            pl.BlockSpec((tm, d), lambda i, j: (i, 0)),
            pl.BlockSpec((1, d), lambda i, j: (0, 0)),
            pl.BlockSpec((d, tn), lambda i, j: (0, j)),
        ],
        out_specs=pl.BlockSpec((tm, tn), lambda i, j: (i, j)),
        out_shape=jax.ShapeDtypeStruct((t, n), BF16),
        scratch_shapes=[pltpu.VMEM((tm, d), BF16)],
        compiler_params=_cparams("parallel", "arbitrary"),
        name="norm_matmul",
    )(x, gain.reshape(1, d), w)


def _proj_residual_kernel(*refs):
    n_in = (len(refs) - 2) // 2
    res_ref, o_ref = refs[2 * n_in], refs[2 * n_in + 1]
    acc = res_ref[...]
    for i in range(n_in):
        acc = acc + _dot(refs[i][...], refs[n_in + i][...])
    o_ref[...] = acc


def proj_residual(acts, weights, res, *, tm):
    t, d = res.shape
    in_specs = [pl.BlockSpec((tm, a.shape[1]), lambda i: (i, 0)) for a in acts]
    in_specs += [pl.BlockSpec(w.shape, lambda i: (0, 0)) for w in weights]
    in_specs += [pl.BlockSpec((tm, d), lambda i: (i, 0))]
    return pl.pallas_call(
        _proj_residual_kernel,
        grid=(t // tm,),
        in_specs=in_specs,
        out_specs=pl.BlockSpec((tm, d), lambda i: (i, 0)),
        out_shape=jax.ShapeDtypeStruct((t, d), F32),
        compiler_params=_cparams("parallel"),
        name="proj_residual",
    )(*acts, *weights, res)


def _swiglu_kernel(x_ref, g_ref, wg_ref, wu_ref, wo_ref, o_ref, h_ref, acc_ref):
    c = pl.program_id(1)

    @pl.when(c == 0)
    def _():
        h_ref[...] = _rms(x_ref[...], g_ref[...]).astype(BF16)
        acc_ref[...] = jnp.zeros_like(acc_ref)

    h = h_ref[...]
    act = _silu(_dot(h, wg_ref[...])) * _dot(h, wu_ref[...])
    acc_ref[...] += _dot(act.astype(BF16), wo_ref[...])

    @pl.when(c == pl.num_programs(1) - 1)
    def _():
        o_ref[...] = x_ref[...] + acc_ref[...]


def swiglu(x, gain, w_in, w_out, *, tm, fc):
    t, d = x.shape
    f = w_out.shape[0]
    nf = f // fc
    return pl.pallas_call(
        _swiglu_kernel,
        grid=(t // tm, nf),
        in_specs=[
            pl.BlockSpec((tm, d), lambda i, c: (i, 0)),
            pl.BlockSpec((1, d), lambda i, c: (0, 0)),
            pl.BlockSpec((d, fc), lambda i, c: (0, c)),
            pl.BlockSpec((d, fc), lambda i, c: (0, c + nf)),
            pl.BlockSpec((fc, d), lambda i, c: (c, 0)),
        ],
        out_specs=pl.BlockSpec((tm, d), lambda i, c: (i, 0)),
        out_shape=jax.ShapeDtypeStruct((t, d), F32),
        scratch_shapes=[pltpu.VMEM((tm, d), BF16), pltpu.VMEM((tm, d), F32)],
        compiler_params=_cparams("parallel", "arbitrary"),
        name="swiglu_dense",
    )(x, gain.reshape(1, d), w_in, w_in, w_out)


def _half_rms(x, gain, low):
    sq = x * x
    s_lo = jnp.sum(jnp.where(low, sq, 0.0), axis=-1, keepdims=True)
    s_hi = jnp.sum(jnp.where(low, 0.0, sq), axis=-1, keepdims=True)
    ms = jnp.where(low, s_lo, s_hi) * (1.0 / DA_HEAD_DIM)
    return x * lax.rsqrt(ms + EPS) * gain


def _diff_attn_kernel(lam_ref, q_ref, k_ref, v_ref, qg_ref, kg_ref, sg_ref, o_ref,
                      kn_ref, vt_ref, s_ref, *, tq, lam_init):
    seq = q_ref.shape[0]
    low = lax.broadcasted_iota(jnp.int32, (1, LANES), 1) < DA_HEAD_DIM
    kn_ref[...] = _half_rms(k_ref[...].astype(F32), kg_ref[...], low).astype(BF16)
    vt_ref[...] = v_ref[...].astype(F32).T.astype(BF16)

    scale = DA_HEAD_DIM ** -0.5 * math.log2(math.e)
    key = lax.broadcasted_iota(jnp.int32, (tq, tq), 0)
    qry = lax.broadcasted_iota(jnp.int32, (tq, tq), 1)
    lam = lam_ref[0]

    for qi in range(seq // tq):
        rows = slice(qi * tq, (qi + 1) * tq)
        qn = _half_rms(q_ref[rows, :].astype(F32), qg_ref[...], low) * scale
        q_maps = (jnp.where(low, qn, 0.0).astype(BF16), jnp.where(low, 0.0, qn).astype(BF16))
        outs = []
        for c in range(2):
            m = jnp.full((1, tq), -jnp.inf, F32)
            for j in range(qi + 1):
                s = _dot_nt(kn_ref[j * tq:(j + 1) * tq, :], q_maps[c])
                if j == qi:
                    s = jnp.where(key <= qry, s, -jnp.inf)
                s_ref[c, j] = s
                m = jnp.maximum(m, jnp.max(s, axis=0, keepdims=True))
            l = jnp.zeros((1, tq), F32)
            acc = jnp.zeros((LANES, tq), F32)
            for j in range(qi + 1):
                p = jnp.exp2(s_ref[c, j] - m)
                l = l + jnp.sum(p, axis=0, keepdims=True)
                acc = acc + _dot(vt_ref[:, j * tq:(j + 1) * tq], p.astype(BF16))
            outs.append(acc / l)
        o = outs[0] - lam * outs[1]
        ms = jnp.mean(o * o, axis=0, keepdims=True)
        o = o * lax.rsqrt(ms + EPS) * sg_ref[...] * (1.0 - lam_init)
        o_ref[rows, :] = o.T.astype(o_ref.dtype)


def diff_attention(z, lam, q_gain, k_gain, subln_gain, *, seq, tq, lam_init):
    t = z.shape[0]
    b = t // seq
    nq = seq // tq
    qg = jnp.concatenate([q_gain, q_gain]).reshape(1, LANES)
    kg = jnp.concatenate([k_gain, k_gain]).reshape(1, LANES)
    sg = subln_gain.astype(F32).reshape(LANES, 1)
    vec = pl.BlockSpec((1, LANES), lambda bi, h: (0, 0))
    head = lambda off: pl.BlockSpec((seq, LANES), lambda bi, h: (bi, off + h))
    return pl.pallas_call(
        functools.partial(_diff_attn_kernel, tq=tq, lam_init=lam_init),
        grid=(b, DA_HEADS),
        in_specs=[pl.BlockSpec(memory_space=pltpu.SMEM),
                  head(0), head(DA_HEADS), head(2 * DA_HEADS),
                  vec, vec, pl.BlockSpec((LANES, 1), lambda bi, h: (0, 0))],
        out_specs=pl.BlockSpec((seq, LANES), lambda bi, h: (bi, h)),
        out_shape=jax.ShapeDtypeStruct((t, DA_HEADS * LANES), BF16),
        scratch_shapes=[pltpu.VMEM((seq, LANES), BF16),
                        pltpu.VMEM((LANES, seq), BF16),
                        pltpu.VMEM((2, nq, tq, tq), F32)],
        compiler_params=_cparams("parallel", "parallel"),
        name="diff_attention",
    )(lam.reshape(1), z, z, z, qg, kg, sg)


def _conv_kernel(val_ref, gate_ref, pval_ref, pgate_ref, dw_ref, b_ref, avg_ref, gg_ref, gb_ref,
                 o_ref, g_ref, sh_ref, *, tr):
    ti = pl.program_id(1)
    halo = pval_ref[...].astype(F32) * jax.nn.sigmoid(pgate_ref[...].astype(F32))
    g_ref[0:CONV_HALO, :] = jnp.where(ti == 0, 0.0, halo)
    g_ref[CONV_HALO:, :] = val_ref[...].astype(F32) * jax.nn.sigmoid(gate_ref[...].astype(F32))

    span = sh_ref.shape[1]
    for b in range(1, SUBLANES):
        sh_ref[b] = g_ref[b:b + span, :]

    first = CONV_HALO - (CONV_KERNEL - 1)
    y = jnp.broadcast_to(b_ref[...], o_ref.shape)
    for k in range(CONV_KERNEL):
        a, b = divmod(first + k, SUBLANES)
        src = g_ref if b == 0 else sh_ref.at[b]
        y = y + src[a * SUBLANES:a * SUBLANES + tr, :] * dw_ref[k:k + 1, :]

    avg = avg_ref[...]

    def group_mean(u):
        a, b, c = _split3(u)
        return _dot(a, avg) + _dot(b, avg) + _dot(c, avg)

    d = y - group_mean(y)
    var = group_mean(d * d)
    yn = d * lax.rsqrt(var + EPS) * gg_ref[...] + gb_ref[...]
    o_ref[...] = _silu(yn).astype(o_ref.dtype)


def conformer_conv(z, dw, dw_bias, gn_gain, gn_bias, *, seq, tr, col0):
    t = z.shape[0]
    c = dw.shape[1]
    b = t // seq
    nt = seq // tr
    hb = tr // CONV_HALO
    grp = c // CONV_GROUPS
    avg = jnp.asarray(np.kron(np.eye(CONV_GROUPS), np.full((grp, grp), 1.0 / grp)), BF16)
    cur = lambda off: pl.BlockSpec((tr, c), lambda bi, ti: (bi * nt + ti, col0 + off))
    prev = lambda off: pl.BlockSpec(
        (CONV_HALO, c), lambda bi, ti: (jnp.maximum((bi * nt + ti) * hb - 1, 0), col0 + off))
    vec = pl.BlockSpec((1, c), lambda bi, ti: (0, 0))
    return pl.pallas_call(
        functools.partial(_conv_kernel, tr=tr),
        grid=(b, nt),
        in_specs=[cur(0), cur(1), prev(0), prev(1),
                  pl.BlockSpec((CONV_KERNEL, c), lambda bi, ti: (0, 0)), vec,
                  pl.BlockSpec((c, c), lambda bi, ti: (0, 0)), vec, vec],
        out_specs=pl.BlockSpec((tr, c), lambda bi, ti: (bi * nt + ti, 0)),
        out_shape=jax.ShapeDtypeStruct((t, c), BF16),
        scratch_shapes=[pltpu.VMEM((CONV_HALO + tr, c), F32),
                        pltpu.VMEM((SUBLANES, CONV_HALO + tr - SUBLANES, c), F32)],
        compiler_params=_cparams("parallel", "arbitrary"),
        name="conformer_conv",
    )(z, z, z, z, dw, dw_bias.reshape(1, c), avg, gn_gain.reshape(1, c), gn_bias.reshape(1, c))


def _hgrn_level_halves(chunk):
    h = 1
    out = []
    while h < chunk:
        out.append(h)
        h *= 2
    return out


def _boundary_cum(cum, half):
    chunk, d = cum.shape
    if 2 * half >= SUBLANES:
        x = cum.reshape(chunk // (2 * half), 2 * half, d)
        return jnp.broadcast_to(x[:, half - 1:half, :], x.shape).reshape(chunk, d)
    x = cum.reshape(chunk // SUBLANES, SUBLANES, d)
    sub = lax.broadcasted_iota(jnp.int32, (1, SUBLANES, 1), 1)
    out = jnp.broadcast_to(x[:, half - 1:half, :], x.shape)
    for start in range(2 * half, SUBLANES, 2 * half):
        pick = jnp.broadcast_to(x[:, start + half - 1:start + half, :], x.shape)
        out = jnp.where(sub >= start, pick, out)
    return out.reshape(chunk, d)


def _block_diag(a, b):
    z = jnp.zeros_like(a)
    return jnp.concatenate([jnp.concatenate([a, z], axis=1), jnp.concatenate([z, b], axis=1)], axis=0)


def _hgrn_kernel(z_ref, tri_ref, loglb_ref, l1mlb_ref, ng_ref, o_ref, st_ref, *, ts, chunk):
    d = o_ref.shape[-1]
    dk = d // HGRN_HEADS

    @pl.when(pl.program_id(1) == 0)
    def _():
        st_ref[...] = jnp.zeros_like(st_ref)

    halves = _hgrn_level_halves(chunk)
    row = lax.broadcasted_iota(jnp.int32, (chunk, 1), 0)
    rr = lax.broadcasted_iota(jnp.int32, (chunk, 2 * chunk), 0)
    cc = lax.broadcasted_iota(jnp.int32, (chunk, 2 * chunk), 1) & (chunk - 1)
    diag = rr == cc
    level_mask = [((rr ^ cc) < 2 * half) & ((rr & half) != 0) & ((cc & half) == 0) for half in halves]
    tri3 = tri_ref[...]

    for ci in range(ts // chunk):
        r0 = ci * chunk
        f_pre = z_ref[r0:r0 + chunk, d:2 * d].astype(F32)
        log_sig = jnp.minimum(f_pre, 0.0) - jnp.log1p(jnp.exp(-jnp.abs(f_pre)))
        a = loglb_ref[...]
        b = l1mlb_ref[...] + log_sig
        mx = jnp.maximum(a, b)
        log_f = mx + jnp.log1p(jnp.exp(-jnp.abs(a - b)))
        f = jnp.exp(log_f)
        cum = _dot(tri3, jnp.concatenate(_split3(log_f), axis=0))
        last = cum[chunk - 1:chunk, :]
        k_all = -jnp.tanh(0.5 * log_f) * (f + 1.0)
        q_all = _silu(z_ref[r0:r0 + chunk, 0:d].astype(F32))

        level_ops = []
        for half in halves:
            right = (row & half) != 0
            if half == 1:
                w = jnp.where(right, f, 1.0)
            else:
                w = jnp.exp(-jnp.abs(cum - _boundary_cum(cum, half)))
            level_ops.append((jnp.where(right, q_all, k_all) * w).astype(BF16))
        q_bf = q_all.astype(BF16)
        k_bf = k_all.astype(BF16)
        q_in = (q_all * jnp.exp(cum)).astype(BF16)
        k_out = (k_all * jnp.exp(last - cum)).astype(BF16)
        decay = jnp.exp(last)

        for hp in range(HGRN_HEADS // 2):
            l0 = slice(2 * hp * dk, (2 * hp + 1) * dk)
            l1 = slice((2 * hp + 1) * dk, (2 * hp + 2) * dk)
            l01 = slice(2 * hp * dk, (2 * hp + 2) * dk)
            scores = jnp.where(diag, _dot_nt(q_bf[:, l01], _block_diag(k_bf[:, l0], k_bf[:, l1])), 0.0)
            for u, mask in zip(level_ops, level_mask):
                scores = jnp.where(mask, _dot_nt(u[:, l01], _block_diag(u[:, l0], u[:, l1])), scores)
            v0 = z_ref[r0:r0 + chunk, 2 * d + 2 * hp * dk:2 * d + (2 * hp + 1) * dk]
            v1 = z_ref[r0:r0 + chunk, 2 * d + (2 * hp + 1) * dk:2 * d + (2 * hp + 2) * dk]
            st0 = st_ref[2 * hp]
            st1 = st_ref[2 * hp + 1]
            o = (_dot_nt(q_in[:, l01], _block_diag(st0.astype(BF16), st1.astype(BF16)))
                 + _dot(scores.astype(BF16), _block_diag(v0, v1)))
            st_ref[2 * hp] = st0 * decay[:, l0] + _dot_tn(v0, k_out[:, l0])
            st_ref[2 * hp + 1] = st1 * decay[:, l1] + _dot_tn(v1, k_out[:, l1])
            gate = _silu(z_ref[r0:r0 + chunk, 3 * d + 2 * hp * dk:3 * d + (2 * hp + 2) * dk].astype(F32))
            for i, ls in enumerate((l0, l1)):
                oh = o[:, i * dk:(i + 1) * dk]
                o_ref[r0:r0 + chunk, ls] = (_rms(oh, ng_ref[:, ls]) * gate[:, i * dk:(i + 1) * dk]
                                            ).astype(o_ref.dtype)


def hgrn2(z, lower_bound, norm_gain, *, seq, ts, chunk=HGRN_CHUNK):
    t = z.shape[0]
    d = z.shape[1] // 4
    b = t // seq
    nt = seq // ts
    assert chunk == d // HGRN_HEADS and chunk & (chunk - 1) == 0
    tri = np.tril(np.ones((chunk, chunk), np.float32))
    tri3 = jnp.asarray(np.concatenate([tri, tri, tri], axis=1), BF16)
    lb = lower_bound.astype(F32).reshape(1, d)
    vec = pl.BlockSpec((1, d), lambda bi, ti: (0, 0))
    return pl.pallas_call(
        functools.partial(_hgrn_kernel, ts=ts, chunk=chunk),
        grid=(b, nt),
        in_specs=[pl.BlockSpec((ts, 4 * d), lambda bi, ti: (bi * nt + ti, 0)),
                  pl.BlockSpec(tri3.shape, lambda bi, ti: (0, 0)), vec, vec, vec],
        out_specs=pl.BlockSpec((ts, d), lambda bi, ti: (bi * nt + ti, 0)),
        out_shape=jax.ShapeDtypeStruct((t, d), BF16),
        scratch_shapes=[pltpu.VMEM((HGRN_HEADS, d // HGRN_HEADS, d // HGRN_HEADS), F32)],
        compiler_params=_cparams("parallel", "arbitrary"),
        name="hgrn2",
    )(z, tri3, jnp.log(lb), jnp.log1p(-lb), norm_gain.reshape(1, d))


def _rows_from_tiles(buf, n, chunks):
    return jnp.concatenate(
        [buf[pl.ds(s, n, stride=SUBLANES), :] for s in range(chunks)], axis=-1)


def _rows_to_tiles(buf, val):
    n = val.shape[0]
    for s in range(val.shape[1] // LANES):
        buf[pl.ds(s, n, stride=SUBLANES), :] = val[:, s * LANES:(s + 1) * LANES]


def _router_kernel(x_ref, g_ref, r_ref, h_ref, info_ref):
    h = _rms(x_ref[...], g_ref[...])
    _rows_to_tiles(h_ref, h)
    h_hi, h_lo, _ = _split3(h)
    logits = _dot(h_hi, r_ref[0]) + _dot(h_hi, r_ref[1]) + _dot(h_lo, r_ref[0])
    lane = lax.broadcasted_iota(jnp.int32, logits.shape, 1).astype(F32)
    logits = jnp.where(lane < N_EXPERTS, logits, -jnp.inf)
    m0 = jnp.max(logits, axis=-1, keepdims=True)
    i0 = jnp.min(jnp.where(logits == m0, lane, float(LANES)), axis=-1, keepdims=True)
    rest = jnp.where(lane == i0, -jnp.inf, logits)
    m1 = jnp.max(rest, axis=-1, keepdims=True)
    i1 = jnp.min(jnp.where(rest == m1, lane, float(LANES)), axis=-1, keepdims=True)
    e1 = jnp.exp(m1 - m0)
    g0 = 1.0 / (1.0 + e1)
    g1 = e1 / (1.0 + e1)
    info = jnp.where(lane == 0.0, i0,
                     jnp.where(lane == 1.0, i1,
                               jnp.where(lane == 2.0, g0, jnp.where(lane == 3.0, g1, 0.0))))
    info_ref[...] = info


def router(x, gain, w_router, *, tm):
    t, d = x.shape
    wpad = jnp.zeros((d, LANES), F32).at[:, :N_EXPERTS].set(w_router.astype(F32))
    w2 = jnp.stack(_split3(wpad)[:2])
    return pl.pallas_call(
        _router_kernel,
        grid=(t // tm,),
        in_specs=[pl.BlockSpec((tm, d), lambda i: (i, 0)),
                  pl.BlockSpec((1, d), lambda i: (0, 0)),
                  pl.BlockSpec((2, d, LANES), lambda i: (0, 0, 0))],
        out_specs=[pl.BlockSpec((tm * SUBLANES, LANES), lambda i: (i, 0)),
                   pl.BlockSpec((tm, LANES), lambda i: (i, 0))],
        out_shape=[jax.ShapeDtypeStruct((t * SUBLANES, LANES), F32),
                   jax.ShapeDtypeStruct((t, LANES), F32)],
        compiler_params=_cparams("parallel"),
        name="router",
    )(x, gain.reshape(1, d), w2)


def _experts_kernel(eid_ref, nused_ref, cur_ref, nxt_ref, h_hbm, wg_ref, wu_ref, wo_ref, y_hbm,
                    xbuf, ybuf, h_ref, acc_ref, sem_x, sem_y, *, bm, gather_steps):
    del eid_ref
    b = pl.program_id(0)
    c = pl.program_id(1)
    last_c = pl.num_programs(1) - 1
    nused = nused_ref[0]
    slot = lax.rem(b, 2)
    per_step = bm // gather_steps
    chunks = h_ref.shape[1] // LANES

    def tile(ref, row):
        return ref.at[pl.ds(pl.multiple_of(row * SUBLANES, SUBLANES), SUBLANES), :]

    def start_gather(idx_ref, dst_slot, first, count):
        def body(i, carry):
            for prio in range(2):
                r = first + 2 * i + prio
                pltpu.make_async_copy(tile(h_hbm, idx_ref[0, 0, r]), tile(xbuf.at[dst_slot], r),
                                      sem_x.at[dst_slot]).start(priority=prio)
            return carry
        lax.fori_loop(0, count // 2, body, 0, unroll=4)

    def wait_gather(dst_slot):
        pltpu.make_async_copy(h_hbm.at[pl.ds(0, bm * SUBLANES), :], xbuf.at[dst_slot],
                              sem_x.at[dst_slot]).wait()

    def start_scatter():
        def body(i, carry):
            for prio in range(2):
                r = 2 * i + prio
                pltpu.make_async_copy(tile(ybuf, r), tile(y_hbm, cur_ref[0, 0, bm + r]),
                                      sem_y).start(priority=prio)
            return carry
        lax.fori_loop(0, bm // 2, body, 0, unroll=4)

    def wait_scatter():
        pltpu.make_async_copy(ybuf, y_hbm.at[pl.ds(0, bm * SUBLANES), :], sem_y).wait()

    @pl.when(b < nused)
    def _():
        @pl.when(c == 0)
        def _():
            @pl.when(b == 0)
            def _():
                start_gather(cur_ref, 0, 0, bm)
            wait_gather(slot)
            h_ref[...] = _rows_from_tiles(xbuf.at[slot], bm, chunks).astype(BF16)
            acc_ref[...] = jnp.zeros_like(acc_ref)

        @pl.when((c < gather_steps) & (b + 1 < nused))
        def _():
            start_gather(nxt_ref, 1 - slot, c * per_step, per_step)

        h = h_ref[...]
        act = _silu(_dot(h, wg_ref[...])) * _dot(h, wu_ref[...])
        acc_ref[...] += _dot(act.astype(BF16), wo_ref[...])

        @pl.when(c == last_c)
        def _():
            @pl.when(b > 0)
            def _():
                wait_scatter()
            _rows_to_tiles(ybuf, acc_ref[...])
            start_scatter()

            @pl.when(b == nused - 1)
            def _():
                wait_scatter()

    @pl.when((b >= nused) & (c == last_c))
    def _():
        start_scatter()
        wait_scatter()


def expert_swiglu(h_tiles, table, block_expert, nused, w_in, w_out, *, bm, fc, out_rows):
    nblk = table.shape[0]
    d = w_in.shape[1]
    f = w_out.shape[1]
    nf = f // fc
    gather_steps = min(4, nf)
    assert d == SUBLANES * LANES and bm % (2 * gather_steps) == 0 and f % fc == 0

    def chunk(b, c, nu):
        return jnp.where(b < nu[0], c, nf - 1)

    grid_spec = pltpu.PrefetchScalarGridSpec(
        num_scalar_prefetch=2,
        grid=(nblk, nf),
        in_specs=[
            pl.BlockSpec((1, 1, 2 * bm), lambda b, c, eid, nu: (b, 0, 0), memory_space=pltpu.SMEM),
            pl.BlockSpec((1, 1, 2 * bm), lambda b, c, eid, nu: (jnp.minimum(b + 1, nblk - 1), 0, 0),
                         memory_space=pltpu.SMEM),
            pl.BlockSpec(memory_space=pl.ANY),
            pl.BlockSpec((None, d, fc), lambda b, c, eid, nu: (eid[b], 0, chunk(b, c, nu))),
            pl.BlockSpec((None, d, fc), lambda b, c, eid, nu: (eid[b], 0, chunk(b, c, nu) + nf)),
            pl.BlockSpec((None, fc, d), lambda b, c, eid, nu: (eid[b], chunk(b, c, nu), 0)),
        ],
        out_specs=pl.BlockSpec(memory_space=pl.ANY),
        scratch_shapes=[pltpu.VMEM((2, bm * SUBLANES, LANES), F32),
                        pltpu.VMEM((bm * SUBLANES, LANES), F32),
                        pltpu.VMEM((bm, d), BF16),
                        pltpu.VMEM((bm, d), F32),
                        pltpu.SemaphoreType.DMA((2,)),
                        pltpu.SemaphoreType.DMA],
    )
    return pl.pallas_call(
        functools.partial(_experts_kernel, bm=bm, gather_steps=gather_steps),
        grid_spec=grid_spec,
        out_shape=jax.ShapeDtypeStruct((out_rows * SUBLANES, LANES), F32),
        compiler_params=_cparams("arbitrary", "arbitrary"),
        name="expert_swiglu",
    )(block_expert, nused, table, table, h_tiles, w_in, w_in, w_out)


def _combine_kernel(x_ref, y0_ref, y1_ref, g_ref, o_ref):
    tm = x_ref.shape[0]
    chunks = x_ref.shape[1] // LANES
    out = x_ref[...]
    for k, y_ref in enumerate((y0_ref, y1_ref)):
        out = out + g_ref[:, TOP_K + k:TOP_K + k + 1] * _rows_from_tiles(y_ref, tm, chunks)
    o_ref[...] = out


def combine(x, y_tiles, info, *, tm):
    t, d = x.shape
    nt = t // tm
    return pl.pallas_call(
        _combine_kernel,
        grid=(nt,),
        in_specs=[pl.BlockSpec((tm, d), lambda i: (i, 0)),
                  pl.BlockSpec((tm * SUBLANES, LANES), lambda i: (i, 0)),
                  pl.BlockSpec((tm * SUBLANES, LANES), lambda i: (i + nt, 0)),
                  pl.BlockSpec((tm, LANES), lambda i: (i, 0))],
        out_specs=pl.BlockSpec((tm, d), lambda i: (i, 0)),
        out_shape=jax.ShapeDtypeStruct((t, d), F32),
        compiler_params=_cparams("parallel"),
        name="moe_combine",
    )(x, y_tiles, y_tiles, info)


def _routing_tables(info, *, bm):
    t = info.shape[0]
    n_pairs = TOP_K * t
    experts = info[:, :TOP_K].astype(jnp.int32).T.reshape(-1)
    onehot = (experts[:, None] == jnp.arange(N_EXPERTS)[None, :]).astype(jnp.int32)
    rank = jnp.sum((jnp.cumsum(onehot, axis=0) - onehot) * onehot, axis=1)
    counts = jnp.sum(onehot, axis=0)
    padded = ((counts + bm - 1) // bm) * bm
    ends = jnp.cumsum(padded)
    slot = (ends - padded)[experts] + rank
    n_slots = n_pairs + N_EXPERTS * bm
    nblk = n_slots // bm
    pair = jnp.full((n_slots,), -1, jnp.int32).at[slot].set(
        jnp.arange(n_pairs, dtype=jnp.int32), unique_indices=True)
    is_pad = pair < 0
    token_row = jnp.where(is_pad, 0, pair % t)
    pad_row = n_pairs + jnp.cumsum(is_pad.astype(jnp.int32)) - 1
    out_row = jnp.where(is_pad, pad_row, pair)
    table = jnp.concatenate([token_row.reshape(nblk, 1, bm), out_row.reshape(nblk, 1, bm)], axis=2)
    nused = (ends[-1] // bm).astype(jnp.int32)
    starts = jnp.arange(nblk, dtype=jnp.int32) * bm
    block_expert = jnp.sum((starts[:, None] >= ends[None, :]).astype(jnp.int32), axis=1)
    block_expert = jnp.minimum(block_expert, N_EXPERTS - 1)
    last_used = jnp.sum(jnp.where(jnp.arange(nblk) == nused - 1, block_expert, 0))
    block_expert = jnp.where(jnp.arange(nblk) < nused, block_expert, last_used).astype(jnp.int32)
    return table, block_expert, nused.reshape(1), n_slots


def moe_layer(x, gain, w_router, w_in, w_out, *, tm_route, bm, fc, tm_combine):
    h_tiles, info = router(x, gain, w_router, tm=tm_route)
    table, block_expert, nused, n_slots = _routing_tables(info, bm=bm)
    y_tiles = expert_swiglu(h_tiles, table, block_expert, nused, w_in, w_out,
                            bm=bm, fc=fc, out_rows=n_slots)
    return combine(x, y_tiles, info, tm=tm_combine)


def kernel(x, mix_norm, ffn_norm, ab_w_in, ab_w_out, da_q_gain, da_k_gain, da_lambda, da_subln,
           conv_dw, conv_dw_bias, conv_gn_gain, conv_gn_bias, hgrn_w_in, hgrn_lb_logits, hgrn_norm,
           hgrn_w_out, ffn_w_in, ffn_w_out, moe_router, moe_w_in, moe_w_out):
    batch, seq, d = x.shape
    depth = mix_norm.shape[0]
    attn_w = DA_HEADS * LANES
    xf = x.reshape(batch * seq, d)

    lb_all = jnp.cumsum(jax.nn.softmax(hgrn_lb_logits.astype(F32), axis=0), axis=0)
    lb_all = lb_all - lb_all[0:1]
    for layer in range(depth):
        j = layer // 2
        if layer % 2 == 0:
            z = norm_matmul(xf, mix_norm[layer], ab_w_in[j].astype(BF16), tm=1024, tn=1280)
            lam_init = 0.8 - 0.6 * math.exp(-0.3 * layer)
            lv = da_lambda[j].astype(F32)
            lam = jnp.exp(jnp.sum(lv[0] * lv[1])) - jnp.exp(jnp.sum(lv[2] * lv[3])) + lam_init
            a = diff_attention(z, lam, da_q_gain[j], da_k_gain[j], da_subln[j],
                               seq=seq, tq=256, lam_init=lam_init)
            c = conformer_conv(z, conv_dw[j], conv_dw_bias[j], conv_gn_gain[j], conv_gn_bias[j],
                               seq=seq, tr=512, col0=3)
            w_o = ab_w_out[j].astype(BF16)
            xf = proj_residual([a, c], [w_o[:attn_w], w_o[attn_w:]], xf, tm=1024)
            xf = swiglu(xf, ffn_norm[layer], ffn_w_in[j].astype(BF16), ffn_w_out[j].astype(BF16),
                        tm=512, fc=1408)
        else:
            z = norm_matmul(xf, mix_norm[layer], hgrn_w_in[j].astype(BF16), tm=1024, tn=2048)
            o = hgrn2(z, lb_all[j], hgrn_norm[j], seq=seq, ts=256)
            xf = proj_residual([o], [hgrn_w_out[j].astype(BF16)], xf, tm=1024)
            xf = moe_layer(xf, ffn_norm[layer], moe_router[j], moe_w_in[j].astype(BF16),
                           moe_w_out[j].astype(BF16), tm_route=1024, bm=1024, fc=512,
                           tm_combine=256)
    return xf.reshape(batch, seq, d)
```

```python
import functools
import math

import numpy as np
import jax
import jax.numpy as jnp
from jax import lax
from jax.experimental import pallas as pl
from jax.experimental.pallas import tpu as pltpu

F32 = jnp.float32
BF16 = jnp.bfloat16

EPS = 1e-6
LANES = 128
SUBLANES = 8
VMEM_LIMIT = 48 * 1024 * 1024

DA_HEADS = 4
DA_HEAD_DIM = 64
CONV_KERNEL = 31
CONV_HALO = 32
CONV_GROUPS = 8
HGRN_HEADS = 8
HGRN_CHUNK = 128
N_EXPERTS = 8
TOP_K = 2


def _cparams(*sem):
    return pltpu.CompilerParams(dimension_semantics=sem, vmem_limit_bytes=VMEM_LIMIT)


def _rms(x, gain):
    ms = jnp.mean(x * x, axis=-1, keepdims=True)
    return x * lax.rsqrt(ms + EPS) * gain


def _silu(x):
    return x * jax.nn.sigmoid(x)


def _split3(x):
    a = x.astype(BF16)
    r = x - a.astype(F32)
    b = r.astype(BF16)
    c = (r - b.astype(F32)).astype(BF16)
    return a, b, c


def _dot(a, b):
    return jnp.dot(a, b, preferred_element_type=F32)


def _dot_nt(a, b):
    return lax.dot_general(a, b, (((1,), (1,)), ((), ())), preferred_element_type=F32)


def _dot_tn(a, b):
    return lax.dot_general(a, b, (((0,), (0,)), ((), ())), preferred_element_type=F32)


def _norm_matmul_kernel(x_ref, g_ref, w_ref, o_ref, h_ref):
    @pl.when(pl.program_id(1) == 0)
    def _():
        h_ref[...] = _rms(x_ref[...], g_ref[...]).astype(BF16)

    o_ref[...] = _dot(h_ref[...], w_ref[...]).astype(o_ref.dtype)


def norm_matmul(x, gain, w, *, tm, tn):
    t, d = x.shape
    n = w.shape[1]
    return pl.pallas_call(
        _norm_matmul_kernel,
        grid=(t // tm, n // tn),
        in_specs=[
            pl.BlockSpec((tm, d), lambda i, j: (i, 0)),
            pl.BlockSpec((1, d), lambda i, j: (0, 0)),
            pl.BlockSpec((d, tn), lambda i, j: (0, j)),
        ],
        out_specs=pl.BlockSpec((tm, tn), lambda i, j: (i, j)),
        out_shape=jax.ShapeDtypeStruct((t, n), BF16),
        scratch_shapes=[pltpu.VMEM((tm, d), BF16)],
        compiler_params=_cparams("parallel", "arbitrary"),
        name="norm_matmul",
    )(x, gain.reshape(1, d), w)


def _proj_residual_kernel(*refs):
    n_in = (len(refs) - 2) // 2
    res_ref, o_ref = refs[2 * n_in], refs[2 * n_in + 1]
    acc = res_ref[...]
    for i in range(n_in):
        acc = acc + _dot(refs[i][...], refs[n_in + i][...])
    o_ref[...] = acc


def proj_residual(acts, weights, res, *, tm):
    t, d = res.shape
    in_specs = [pl.BlockSpec((tm, a.shape[1]), lambda i: (i, 0)) for a in acts]
    in_specs += [pl.BlockSpec(w.shape, lambda i: (0, 0)) for w in weights]
    in_specs += [pl.BlockSpec((tm, d), lambda i: (i, 0))]
    return pl.pallas_call(
        _proj_residual_kernel,
        grid=(t // tm,),
        in_specs=in_specs,
        out_specs=pl.BlockSpec((tm, d), lambda i: (i, 0)),
        out_shape=jax.ShapeDtypeStruct((t, d), F32),
        compiler_params=_cparams("parallel"),
        name="proj_residual",
    )(*acts, *weights, res)


def _swiglu_kernel(x_ref, g_ref, wg_ref, wu_ref, wo_ref, o_ref, h_ref, acc_ref):
    c = pl.program_id(1)

    @pl.when(c == 0)
    def _():
        h_ref[...] = _rms(x_ref[...], g_ref[...]).astype(BF16)
        acc_ref[...] = jnp.zeros_like(acc_ref)

    h = h_ref[...]
    act = _silu(_dot(h, wg_ref[...])) * _dot(h, wu_ref[...])
    acc_ref[...] += _dot(act.astype(BF16), wo_ref[...])

    @pl.when(c == pl.num_programs(1) - 1)
    def _():
        o_ref[...] = x_ref[...] + acc_ref[...]


def swiglu(x, gain, w_in, w_out, *, tm, fc):
    t, d = x.shape
    f = w_out.shape[0]
    nf = f // fc
    return pl.pallas_call(
        _swiglu_kernel,
        grid=(t // tm, nf),
        in_specs=[
            pl.BlockSpec((tm, d), lambda i, c: (i, 0)),
            pl.BlockSpec((1, d), lambda i, c: (0, 0)),
            pl.BlockSpec((d, fc), lambda i, c: (0, c)),
            pl.BlockSpec((d, fc), lambda i, c: (0, c + nf)),
            pl.BlockSpec((fc, d), lambda i, c: (c, 0)),
        ],
        out_specs=pl.BlockSpec((tm, d), lambda i, c: (i, 0)),
        out_shape=jax.ShapeDtypeStruct((t, d), F32),
        scratch_shapes=[pltpu.VMEM((tm, d), BF16), pltpu.VMEM((tm, d), F32)],
        compiler_params=_cparams("parallel", "arbitrary"),
        name="swiglu_dense",
    )(x, gain.reshape(1, d), w_in, w_in, w_out)


def _half_rms(x, gain, low):
    sq = x * x
    s_lo = jnp.sum(jnp.where(low, sq, 0.0), axis=-1, keepdims=True)
    s_hi = jnp.sum(jnp.where(low, 0.0, sq), axis=-1, keepdims=True)
    ms = jnp.where(low, s_lo, s_hi) * (1.0 / DA_HEAD_DIM)
    return x * lax.rsqrt(ms + EPS) * gain


def _diff_attn_kernel(lam_ref, q_ref, k_ref, v_ref, qg_ref, kg_ref, sg_ref, o_ref,
                      kn_ref, vt_ref, s_ref, *, tq, lam_init):
    seq = q_ref.shape[0]
    low = lax.broadcasted_iota(jnp.int32, (1, LANES), 1) < DA_HEAD_DIM
    kn_ref[...] = _half_rms(k_ref[...].astype(F32), kg_ref[...], low).astype(BF16)
    vt_ref[...] = v_ref[...].astype(F32).T.astype(BF16)

    scale = DA_HEAD_DIM ** -0.5 * math.log2(math.e)
    key = lax.broadcasted_iota(jnp.int32, (tq, tq), 0)
    qry = lax.broadcasted_iota(jnp.int32, (tq, tq), 1)
    lam = lam_ref[0]

    for qi in range(seq // tq):
        rows = slice(qi * tq, (qi + 1) * tq)
        qn = _half_rms(q_ref[rows, :].astype(F32), qg_ref[...], low) * scale
        q_maps = (jnp.where(low, qn, 0.0).astype(BF16), jnp.where(low, 0.0, qn).astype(BF16))
        outs = []
        for c in range(2):
            m = jnp.full((1, tq), -jnp.inf, F32)
            for j in range(qi + 1):
                s = _dot_nt(kn_ref[j * tq:(j + 1) * tq, :], q_maps[c])
                if j == qi:
                    s = jnp.where(key <= qry, s, -jnp.inf)
                s_ref[c, j] = s
                m = jnp.maximum(m, jnp.max(s, axis=0, keepdims=True))
            l = jnp.zeros((1, tq), F32)
            acc = jnp.zeros((LANES, tq), F32)
            for j in range(qi + 1):
                p = jnp.exp2(s_ref[c, j] - m)
                l = l + jnp.sum(p, axis=0, keepdims=True)
                acc = acc + _dot(vt_ref[:, j * tq:(j + 1) * tq], p.astype(BF16))
            outs.append(acc / l)
        o = outs[0] - lam * outs[1]
        ms = jnp.mean(o * o, axis=0, keepdims=True)
        o = o * lax.rsqrt(ms + EPS) * sg_ref[...] * (1.0 - lam_init)
        o_ref[rows, :] = o.T.astype(o_ref.dtype)


def diff_attention(z, lam, q_gain, k_gain, subln_gain, *, seq, tq, lam_init):
    t = z.shape[0]
    b = t // seq
    nq = seq // tq
    qg = jnp.concatenate([q_gain, q_gain]).reshape(1, LANES)
    kg = jnp.concatenate([k_gain, k_gain]).reshape(1, LANES)
    sg = subln_gain.astype(F32).reshape(LANES, 1)
    vec = pl.BlockSpec((1, LANES), lambda bi, h: (0, 0))
    head = lambda off: pl.BlockSpec((seq, LANES), lambda bi, h: (bi, off + h))
    return pl.pallas_call(
        functools.partial(_diff_attn_kernel, tq=tq, lam_init=lam_init),
        grid=(b, DA_HEADS),
        in_specs=[pl.BlockSpec(memory_space=pltpu.SMEM),
                  head(0), head(DA_HEADS), head(2 * DA_HEADS),
                  vec, vec, pl.BlockSpec((LANES, 1), lambda bi, h: (0, 0))],
        out_specs=pl.BlockSpec((seq, LANES), lambda bi, h: (bi, h)),
        out_shape=jax.ShapeDtypeStruct((t, DA_HEADS * LANES), BF16),
        scratch_shapes=[pltpu.VMEM((seq, LANES), BF16),
                        pltpu.VMEM((LANES, seq), BF16),
                        pltpu.VMEM((2, nq, tq, tq), F32)],
        compiler_params=_cparams("parallel", "parallel"),
        name="diff_attention",
    )(lam.reshape(1), z, z, z, qg, kg, sg)


def _conv_kernel(val_ref, gate_ref, pval_ref, pgate_ref, dw_ref, b_ref, avg_ref, gg_ref, gb_ref,
                 o_ref, g_ref, sh_ref, *, tr):
    ti = pl.program_id(1)
    halo = pval_ref[...].astype(F32) * jax.nn.sigmoid(pgate_ref[...].astype(F32))
    g_ref[0:CONV_HALO, :] = jnp.where(ti == 0, 0.0, halo)
    g_ref[CONV_HALO:, :] = val_ref[...].astype(F32) * jax.nn.sigmoid(gate_ref[...].astype(F32))

    span = sh_ref.shape[1]
    for b in range(1, SUBLANES):
        sh_ref[b] = g_ref[b:b + span, :]

    first = CONV_HALO - (CONV_KERNEL - 1)
    y = jnp.broadcast_to(b_ref[...], o_ref.shape)
    for k in range(CONV_KERNEL):
        a, b = divmod(first + k, SUBLANES)
        src = g_ref if b == 0 else sh_ref.at[b]
        y = y + src[a * SUBLANES:a * SUBLANES + tr, :] * dw_ref[k:k + 1, :]

    avg = avg_ref[...]

    def group_mean(u):
        a, b, c = _split3(u)
        return _dot(a, avg) + _dot(b, avg) + _dot(c, avg)

    d = y - group_mean(y)
    var = group_mean(d * d)
    yn = d * lax.rsqrt(var + EPS) * gg_ref[...] + gb_ref[...]
    o_ref[...] = _silu(yn).astype(o_ref.dtype)


def conformer_conv(z, dw, dw_bias, gn_gain, gn_bias, *, seq, tr, col0):
    t = z.shape[0]
    c = dw.shape[1]
    b = t // seq
    nt = seq // tr
    hb = tr // CONV_HALO
    grp = c // CONV_GROUPS
    avg = jnp.asarray(np.kron(np.eye(CONV_GROUPS), np.full((grp, grp), 1.0 / grp)), BF16)
    cur = lambda off: pl.BlockSpec((tr, c), lambda bi, ti: (bi * nt + ti, col0 + off))
    prev = lambda off: pl.BlockSpec(
        (CONV_HALO, c), lambda bi, ti: (jnp.maximum((bi * nt + ti) * hb - 1, 0), col0 + off))
    vec = pl.BlockSpec((1, c), lambda bi, ti: (0, 0))
    return pl.pallas_call(
        functools.partial(_conv_kernel, tr=tr),
        grid=(b, nt),
        in_specs=[cur(0), cur(1), prev(0), prev(1),
                  pl.BlockSpec((CONV_KERNEL, c), lambda bi, ti: (0, 0)), vec,
                  pl.BlockSpec((c, c), lambda bi, ti: (0, 0)), vec, vec],
        out_specs=pl.BlockSpec((tr, c), lambda bi, ti: (bi * nt + ti, 0)),
        out_shape=jax.ShapeDtypeStruct((t, c), BF16),
        scratch_shapes=[pltpu.VMEM((CONV_HALO + tr, c), F32),
                        pltpu.VMEM((SUBLANES, CONV_HALO + tr - SUBLANES, c), F32)],
        compiler_params=_cparams("parallel", "arbitrary"),
        name="conformer_conv",
    )(z, z, z, z, dw, dw_bias.reshape(1, c), avg, gn_gain.reshape(1, c), gn_bias.reshape(1, c))


def _hgrn_level_halves(chunk):
    h = 1
    out = []
    while h < chunk:
        out.append(h)
        h *= 2
    return out


def _boundary_cum(cum, half):
    chunk, d = cum.shape
    if 2 * half >= SUBLANES:
        x = cum.reshape(chunk // (2 * half), 2 * half, d)
        return jnp.broadcast_to(x[:, half - 1:half, :], x.shape).reshape(chunk, d)
    x = cum.reshape(chunk // SUBLANES, SUBLANES, d)
    sub = lax.broadcasted_iota(jnp.int32, (1, SUBLANES, 1), 1)
    out = jnp.broadcast_to(x[:, half - 1:half, :], x.shape)
    for start in range(2 * half, SUBLANES, 2 * half):
        pick = jnp.broadcast_to(x[:, start + half - 1:start + half, :], x.shape)
        out = jnp.where(sub >= start, pick, out)
    return out.reshape(chunk, d)


def _block_diag(a, b):
    z = jnp.zeros_like(a)
    return jnp.concatenate([jnp.concatenate([a, z], axis=1), jnp.concatenate([z, b], axis=1)], axis=0)


def _right_rows_q_left_rows_k(q, k, half, row):
    if half >= SUBLANES:
        n = q.shape[0] // half
        return jnp.concatenate([(q if i % 2 else k)[i * half:(i + 1) * half] for i in range(n)], axis=0)
    return jnp.where((row & half) != 0, q, k)


def _hgrn_kernel(z_ref, tri_ref, loglb_ref, l1mlb_ref, ng_ref, o_ref, st_ref, *, ts, chunk):
    d = o_ref.shape[-1]
    dk = d // HGRN_HEADS

    @pl.when(pl.program_id(1) == 0)
    def _():
        st_ref[...] = jnp.zeros_like(st_ref)

    halves = _hgrn_level_halves(chunk)
    row = lax.broadcasted_iota(jnp.int32, (chunk, 1), 0)
    rr = lax.broadcasted_iota(jnp.int32, (chunk, 2 * chunk), 0)
    cc = lax.broadcasted_iota(jnp.int32, (chunk, 2 * chunk), 1) & (chunk - 1)
    diag = rr == cc
    level_mask = [((rr ^ cc) < 2 * half) & ((rr & half) != 0) & ((cc & half) == 0) for half in halves]
    tri3 = tri_ref[...]

    for ci in range(ts // chunk):
        r0 = ci * chunk
        f_pre = z_ref[r0:r0 + chunk, d:2 * d].astype(F32)
        log_sig = jnp.minimum(f_pre, 0.0) - jnp.log1p(jnp.exp(-jnp.abs(f_pre)))
        a = loglb_ref[...]
        b = l1mlb_ref[...] + log_sig
        mx = jnp.maximum(a, b)
        log_f = mx + jnp.log1p(jnp.exp(-jnp.abs(a - b)))
        f = jnp.exp(log_f)
        cum = _dot(tri3, jnp.concatenate(_split3(log_f), axis=0))
        last = cum[chunk - 1:chunk, :]
        k_all = -jnp.tanh(0.5 * log_f) * (f + 1.0)
        q_all = _silu(z_ref[r0:r0 + chunk, 0:d].astype(F32))

        level_ops = []
        for half in halves:
            if half == 1:
                w = jnp.where((row & half) != 0, f, 1.0)
            else:
                w = jnp.exp(-jnp.abs(cum - _boundary_cum(cum, half)))
            level_ops.append((_right_rows_q_left_rows_k(q_all, k_all, half, row) * w).astype(BF16))
        q_bf = q_all.astype(BF16)
        k_bf = k_all.astype(BF16)
        q_in = (q_all * jnp.exp(cum)).astype(BF16)
        k_out = (k_all * jnp.exp(last - cum)).astype(BF16)
        decay = jnp.exp(last)

        for hp in range(HGRN_HEADS // 2):
            l0 = slice(2 * hp * dk, (2 * hp + 1) * dk)
            l1 = slice((2 * hp + 1) * dk, (2 * hp + 2) * dk)
            l01 = slice(2 * hp * dk, (2 * hp + 2) * dk)
            scores = jnp.where(diag, _dot_nt(q_bf[:, l01], _block_diag(k_bf[:, l0], k_bf[:, l1])), 0.0)
            for u, mask in zip(level_ops, level_mask):
                scores = jnp.where(mask, _dot_nt(u[:, l01], _block_diag(u[:, l0], u[:, l1])), scores)
            v0 = z_ref[r0:r0 + chunk, 2 * d + 2 * hp * dk:2 * d + (2 * hp + 1) * dk]
            v1 = z_ref[r0:r0 + chunk, 2 * d + (2 * hp + 1) * dk:2 * d + (2 * hp + 2) * dk]
            st0 = st_ref[2 * hp]
            st1 = st_ref[2 * hp + 1]
            o = (_dot_nt(q_in[:, l01], _block_diag(st0.astype(BF16), st1.astype(BF16)))
                 + _dot(scores.astype(BF16), _block_diag(v0, v1)))
            st_ref[2 * hp] = st0 * decay[:, l0] + _dot_tn(v0, k_out[:, l0])
            st_ref[2 * hp + 1] = st1 * decay[:, l1] + _dot_tn(v1, k_out[:, l1])
            gate = _silu(z_ref[r0:r0 + chunk, 3 * d + 2 * hp * dk:3 * d + (2 * hp + 2) * dk].astype(F32))
            for i, ls in enumerate((l0, l1)):
                oh = o[:, i * dk:(i + 1) * dk]
                o_ref[r0:r0 + chunk, ls] = (_rms(oh, ng_ref[:, ls]) * gate[:, i * dk:(i + 1) * dk]
                                            ).astype(o_ref.dtype)


def hgrn2(z, lower_bound, norm_gain, *, seq, ts, chunk=HGRN_CHUNK):
    t = z.shape[0]
    d = z.shape[1] // 4
    b = t // seq
    nt = seq // ts
    assert chunk == d // HGRN_HEADS and chunk & (chunk - 1) == 0
    tri = np.tril(np.ones((chunk, chunk), np.float32))
    tri3 = jnp.asarray(np.concatenate([tri, tri, tri], axis=1), BF16)
    lb = lower_bound.astype(F32).reshape(1, d)
    vec = pl.BlockSpec((1, d), lambda bi, ti: (0, 0))
    return pl.pallas_call(
        functools.partial(_hgrn_kernel, ts=ts, chunk=chunk),
        grid=(b, nt),
        in_specs=[pl.BlockSpec((ts, 4 * d), lambda bi, ti: (bi * nt + ti, 0)),
                  pl.BlockSpec(tri3.shape, lambda bi, ti: (0, 0)), vec, vec, vec],
        out_specs=pl.BlockSpec((ts, d), lambda bi, ti: (bi * nt + ti, 0)),
        out_shape=jax.ShapeDtypeStruct((t, d), BF16),
        scratch_shapes=[pltpu.VMEM((HGRN_HEADS, d // HGRN_HEADS, d // HGRN_HEADS), F32)],
        compiler_params=_cparams("parallel", "arbitrary"),
        name="hgrn2",
    )(z, tri3, jnp.log(lb), jnp.log1p(-lb), norm_gain.reshape(1, d))


def _rows_from_tiles(buf, n, chunks):
    return jnp.concatenate(
        [buf[pl.ds(s, n, stride=SUBLANES), :] for s in range(chunks)], axis=-1)


def _rows_to_tiles(buf, val):
    n = val.shape[0]
    for s in range(val.shape[1] // LANES):
        buf[pl.ds(s, n, stride=SUBLANES), :] = val[:, s * LANES:(s + 1) * LANES]


def _router_kernel(x_ref, g_ref, r_ref, h_ref, info_ref):
    h = _rms(x_ref[...], g_ref[...])
    _rows_to_tiles(h_ref, h)
    h_hi, h_lo, _ = _split3(h)
    logits = _dot(h_hi, r_ref[0]) + _dot(h_hi, r_ref[1]) + _dot(h_lo, r_ref[0])
    lane = lax.broadcasted_iota(jnp.int32, logits.shape, 1).astype(F32)
    logits = jnp.where(lane < N_EXPERTS, logits, -jnp.inf)
    m0 = jnp.max(logits, axis=-1, keepdims=True)
    i0 = jnp.min(jnp.where(logits == m0, lane, float(LANES)), axis=-1, keepdims=True)
    rest = jnp.where(lane == i0, -jnp.inf, logits)
    m1 = jnp.max(rest, axis=-1, keepdims=True)
    i1 = jnp.min(jnp.where(rest == m1, lane, float(LANES)), axis=-1, keepdims=True)
    e1 = jnp.exp(m1 - m0)
    g0 = 1.0 / (1.0 + e1)
    g1 = e1 / (1.0 + e1)
    info = jnp.where(lane == 0.0, i0,
                     jnp.where(lane == 1.0, i1,
                               jnp.where(lane == 2.0, g0, jnp.where(lane == 3.0, g1, 0.0))))
    info_ref[...] = info


def router(x, gain, w_router, *, tm):
    t, d = x.shape
    wpad = jnp.zeros((d, LANES), F32).at[:, :N_EXPERTS].set(w_router.astype(F32))
    w2 = jnp.stack(_split3(wpad)[:2])
    return pl.pallas_call(
        _router_kernel,
        grid=(t // tm,),
        in_specs=[pl.BlockSpec((tm, d), lambda i: (i, 0)),
                  pl.BlockSpec((1, d), lambda i: (0, 0)),
                  pl.BlockSpec((2, d, LANES), lambda i: (0, 0, 0))],
        out_specs=[pl.BlockSpec((tm * SUBLANES, LANES), lambda i: (i, 0)),
                   pl.BlockSpec((tm, LANES), lambda i: (i, 0))],
        out_shape=[jax.ShapeDtypeStruct((t * SUBLANES, LANES), F32),
                   jax.ShapeDtypeStruct((t, LANES), F32)],
        compiler_params=_cparams("parallel"),
        name="router",
    )(x, gain.reshape(1, d), w2)


def _experts_kernel(eid_ref, nused_ref, cur_ref, nxt_ref, h_hbm, wg_ref, wu_ref, wo_ref, y_hbm,
                    xbuf, ybuf, h_ref, acc_ref, sem_x, sem_y, *, bm, gather_steps):
    del eid_ref
    b = pl.program_id(0)
    c = pl.program_id(1)
    last_c = pl.num_programs(1) - 1
    nused = nused_ref[0]
    slot = lax.rem(b, 2)
    per_step = bm // gather_steps
    chunks = h_ref.shape[1] // LANES

    def tile(ref, row):
        if isinstance(row, int):
            return ref.at[pl.ds(row * SUBLANES, SUBLANES), :]
        return ref.at[pl.ds(pl.multiple_of(row * SUBLANES, SUBLANES), SUBLANES), :]

    def for_rows(first, count, fn, unrolled):
        if unrolled:
            for i in range(count):
                fn(first + i, i % 2)
        else:
            def body(i, carry):
                for prio in range(2):
                    fn(first + 2 * i + prio, prio)
                return carry
            lax.fori_loop(0, count // 2, body, 0, unroll=4)

    def start_gather(idx_ref, dst_slot, first, count, unrolled):
        def one(r, prio):
            pltpu.make_async_copy(tile(h_hbm, idx_ref[0, 0, r]), tile(xbuf.at[dst_slot], r),
                                  sem_x.at[dst_slot]).start(priority=prio)
        for_rows(first, count, one, unrolled)

    def wait_gather(dst_slot):
        pltpu.make_async_copy(h_hbm.at[pl.ds(0, bm * SUBLANES), :], xbuf.at[dst_slot],
                              sem_x.at[dst_slot]).wait()

    def start_scatter(unrolled):
        def one(r, prio):
            pltpu.make_async_copy(tile(ybuf, r), tile(y_hbm, cur_ref[0, 0, bm + r]),
                                  sem_y).start(priority=prio)
        for_rows(0, bm, one, unrolled)

    def wait_scatter():
        pltpu.make_async_copy(ybuf, y_hbm.at[pl.ds(0, bm * SUBLANES), :], sem_y).wait()

    @pl.when(b < nused)
    def _():
        @pl.when(c == 0)
        def _():
            @pl.when(b == 0)
            def _():
                start_gather(cur_ref, 0, 0, bm, False)
            wait_gather(slot)
            h_ref[...] = _rows_from_tiles(xbuf.at[slot], bm, chunks).astype(BF16)
            acc_ref[...] = jnp.zeros_like(acc_ref)

        for step in range(gather_steps):
            @pl.when((c == step) & (b + 1 < nused))
            def _():
                start_gather(nxt_ref, 1 - slot, step * per_step, per_step, True)

        h = h_ref[...]
        act = _silu(_dot(h, wg_ref[...])) * _dot(h, wu_ref[...])
        acc_ref[...] += _dot(act.astype(BF16), wo_ref[...])

        @pl.when(c == last_c)
        def _():
            @pl.when(b > 0)
            def _():
                wait_scatter()
            _rows_to_tiles(ybuf, acc_ref[...])
            start_scatter(True)

            @pl.when(b == nused - 1)
            def _():
                wait_scatter()

    @pl.when((b >= nused) & (c == last_c))
    def _():
        start_scatter(False)
        wait_scatter()


def expert_swiglu(h_tiles, table, block_expert, nused, w_in, w_out, *, bm, fc, out_rows):
    nblk = table.shape[0]
    d = w_in.shape[1]
    f = w_out.shape[1]
    nf = f // fc
    gather_steps = min(4, nf)
    assert d == SUBLANES * LANES and bm % (2 * gather_steps) == 0 and f % fc == 0

    def chunk(b, c, nu):
        return jnp.where(b < nu[0], c, nf - 1)

    grid_spec = pltpu.PrefetchScalarGridSpec(
        num_scalar_prefetch=2,
        grid=(nblk, nf),
        in_specs=[
            pl.BlockSpec((1, 1, 2 * bm), lambda b, c, eid, nu: (b, 0, 0), memory_space=pltpu.SMEM),
            pl.BlockSpec((1, 1, 2 * bm), lambda b, c, eid, nu: (jnp.minimum(b + 1, nblk - 1), 0, 0),
                         memory_space=pltpu.SMEM),
            pl.BlockSpec(memory_space=pl.ANY),
            pl.BlockSpec((None, d, fc), lambda b, c, eid, nu: (eid[b], 0, chunk(b, c, nu))),
            pl.BlockSpec((None, d, fc), lambda b, c, eid, nu: (eid[b], 0, chunk(b, c, nu) + nf)),
            pl.BlockSpec((None, fc, d), lambda b, c, eid, nu: (eid[b], chunk(b, c, nu), 0)),
        ],
        out_specs=pl.BlockSpec(memory_space=pl.ANY),
        scratch_shapes=[pltpu.VMEM((2, bm * SUBLANES, LANES), F32),
                        pltpu.VMEM((bm * SUBLANES, LANES), F32),
                        pltpu.VMEM((bm, d), BF16),
                        pltpu.VMEM((bm, d), F32),
                        pltpu.SemaphoreType.DMA((2,)),
                        pltpu.SemaphoreType.DMA],
    )
    return pl.pallas_call(
        functools.partial(_experts_kernel, bm=bm, gather_steps=gather_steps),
        grid_spec=grid_spec,
        out_shape=jax.ShapeDtypeStruct((out_rows * SUBLANES, LANES), F32),
        compiler_params=_cparams("arbitrary", "arbitrary"),
        name="expert_swiglu",
    )(block_expert, nused, table, table, h_tiles, w_in, w_in, w_out)


def _combine_kernel(x_ref, y0_ref, y1_ref, g_ref, o_ref):
    tm = x_ref.shape[0]
    chunks = x_ref.shape[1] // LANES
    out = x_ref[...]
    for k, y_ref in enumerate((y0_ref, y1_ref)):
        out = out + g_ref[:, TOP_K + k:TOP_K + k + 1] * _rows_from_tiles(y_ref, tm, chunks)
    o_ref[...] = out


def combine(x, y_tiles, info, *, tm):
    t, d = x.shape
    nt = t // tm
    return pl.pallas_call(
        _combine_kernel,
        grid=(nt,),
        in_specs=[pl.BlockSpec((tm, d), lambda i: (i, 0)),
                  pl.BlockSpec((tm * SUBLANES, LANES), lambda i: (i, 0)),
                  pl.BlockSpec((tm * SUBLANES, LANES), lambda i: (i + nt, 0)),
                  pl.BlockSpec((tm, LANES), lambda i: (i, 0))],
        out_specs=pl.BlockSpec((tm, d), lambda i: (i, 0)),
        out_shape=jax.ShapeDtypeStruct((t, d), F32),
        compiler_params=_cparams("parallel"),
        name="moe_combine",
    )(x, y_tiles, y_tiles, info)


def _routing_tables(info, *, bm):
    t = info.shape[0]
    n_pairs = TOP_K * t
    experts = info[:, :TOP_K].astype(jnp.int32).T.reshape(-1)
    onehot = (experts[:, None] == jnp.arange(N_EXPERTS)[None, :]).astype(jnp.int32)
    rank = jnp.sum((jnp.cumsum(onehot, axis=0) - onehot) * onehot, axis=1)
    counts = jnp.sum(onehot, axis=0)
    padded = ((counts + bm - 1) // bm) * bm
    ends = jnp.cumsum(padded)
    starts_e = ends - padded
    slot = starts_e[experts] + rank
    n_pads = N_EXPERTS * bm
    n_slots = n_pairs + n_pads
    nblk = n_slots // bm
    pads_e = padded - counts
    pad_ends = jnp.cumsum(pads_e)
    i = jnp.arange(n_pads, dtype=jnp.int32)
    e_i = jnp.sum((i[:, None] >= pad_ends[None, :]).astype(jnp.int32), axis=1)
    e_c = jnp.minimum(e_i, N_EXPERTS - 1)
    pad_slot = jnp.where(e_i < N_EXPERTS,
                         (starts_e + counts)[e_c] + i - (pad_ends - pads_e)[e_c],
                         ends[-1] + i - pad_ends[-1])
    _, pair = lax.sort_key_val(jnp.concatenate([slot, pad_slot]),
                               jnp.concatenate([jnp.arange(n_pairs, dtype=jnp.int32), -1 - i]))
    is_pad = pair < 0
    token_row = jnp.where(is_pad, 0, pair % t)
    out_row = jnp.where(is_pad, n_pairs - 1 - pair, pair)
    table = jnp.concatenate([token_row.reshape(nblk, 1, bm), out_row.reshape(nblk, 1, bm)], axis=2)
    nused = (ends[-1] // bm).astype(jnp.int32)
    starts = jnp.arange(nblk, dtype=jnp.int32) * bm
    block_expert = jnp.sum((starts[:, None] >= ends[None, :]).astype(jnp.int32), axis=1)
    block_expert = jnp.minimum(block_expert, N_EXPERTS - 1)
    last_used = jnp.sum(jnp.where(jnp.arange(nblk) == nused - 1, block_expert, 0))
    block_expert = jnp.where(jnp.arange(nblk) < nused, block_expert, last_used).astype(jnp.int32)
    return table, block_expert, nused.reshape(1), n_slots


def moe_layer(x, gain, w_router, w_in, w_out, *, tm_route, bm, fc, tm_combine):
    h_tiles, info = router(x, gain, w_router, tm=tm_route)
    table, block_expert, nused, n_slots = _routing_tables(info, bm=bm)
    y_tiles = expert_swiglu(h_tiles, table, block_expert, nused, w_in, w_out,
                            bm=bm, fc=fc, out_rows=n_slots)
    return combine(x, y_tiles, info, tm=tm_combine)


def kernel(x, mix_norm, ffn_norm, ab_w_in, ab_w_out, da_q_gain, da_k_gain, da_lambda, da_subln,
           conv_dw, conv_dw_bias, conv_gn_gain, conv_gn_bias, hgrn_w_in, hgrn_lb_logits, hgrn_norm,
           hgrn_w_out, ffn_w_in, ffn_w_out, moe_router, moe_w_in, moe_w_out):
    batch, seq, d = x.shape
    depth = mix_norm.shape[0]
    attn_w = DA_HEADS * LANES
    xf = x.reshape(batch * seq, d)

    lb_all = jnp.cumsum(jax.nn.softmax(hgrn_lb_logits.astype(F32), axis=0), axis=0)
    lb_all = lb_all - lb_all[0:1]
    for layer in range(depth):
        j = layer // 2
        if layer % 2 == 0:
            z = norm_matmul(xf, mix_norm[layer], ab_w_in[j].astype(BF16), tm=1024, tn=1280)
            lam_init = 0.8 - 0.6 * math.exp(-0.3 * layer)
            lv = da_lambda[j].astype(F32)
            lam = jnp.exp(jnp.sum(lv[0] * lv[1])) - jnp.exp(jnp.sum(lv[2] * lv[3])) + lam_init
            a = diff_attention(z, lam, da_q_gain[j], da_k_gain[j], da_subln[j],
                               seq=seq, tq=256, lam_init=lam_init)
            c = conformer_conv(z, conv_dw[j], conv_dw_bias[j], conv_gn_gain[j], conv_gn_bias[j],
                               seq=seq, tr=512, col0=3)
            w_o = ab_w_out[j].astype(BF16)
            xf = proj_residual([a, c], [w_o[:attn_w], w_o[attn_w:]], xf, tm=1024)
            xf = swiglu(xf, ffn_norm[layer], ffn_w_in[j].astype(BF16), ffn_w_out[j].astype(BF16),
                        tm=512, fc=1408)
        else:
            z = norm_matmul(xf, mix_norm[layer], hgrn_w_in[j].astype(BF16), tm=1024, tn=2048)
            o = hgrn2(z, lb_all[j], hgrn_norm[j], seq=seq, ts=256)
            xf = proj_residual([o], [hgrn_w_out[j].astype(BF16)], xf, tm=1024)
            xf = moe_layer(xf, ffn_norm[layer], moe_router[j], moe_w_in[j].astype(BF16),
                           moe_w_out[j].astype(BF16), tm_route=1024, bm=1024, fc=512,
                           tm_combine=256)
    return xf.reshape(batch, seq, d)
```

```python
import functools
import math

import numpy as np
import jax
import jax.numpy as jnp
from jax import lax
from jax.experimental import pallas as pl
from jax.experimental.pallas import tpu as pltpu

F32 = jnp.float32
BF16 = jnp.bfloat16

EPS = 1e-6
LANES = 128
SUBLANES = 8
VMEM_LIMIT = 48 * 1024 * 1024

DA_HEADS = 4
DA_HEAD_DIM = 64
CONV_KERNEL = 31
CONV_HALO = 32
CONV_GROUPS = 8
HGRN_HEADS = 8
HGRN_CHUNK = 128
N_EXPERTS = 8
TOP_K = 2


def _cparams(*sem):
    return pltpu.CompilerParams(dimension_semantics=sem, vmem_limit_bytes=VMEM_LIMIT)


def _rms(x, gain):
    ms = jnp.mean(x * x, axis=-1, keepdims=True)
    return x * lax.rsqrt(ms + EPS) * gain


def _silu(x):
    return x * jax.nn.sigmoid(x)


def _split3(x):
    a = x.astype(BF16)
    r = x - a.astype(F32)
    b = r.astype(BF16)
    c = (r - b.astype(F32)).astype(BF16)
    return a, b, c


def _dot(a, b):
    return jnp.dot(a, b, preferred_element_type=F32)


def _dot_nt(a, b):
    return lax.dot_general(a, b, (((1,), (1,)), ((), ())), preferred_element_type=F32)


def _dot_tn(a, b):
    return lax.dot_general(a, b, (((0,), (0,)), ((), ())), preferred_element_type=F32)


def _norm_matmul_kernel(x_ref, g_ref, w_ref, o_ref, h_ref):
    @pl.when(pl.program_id(1) == 0)
    def _():
        h_ref[...] = _rms(x_ref[...], g_ref[...]).astype(BF16)

    o_ref[...] = _dot(h_ref[...], w_ref[...]).astype(o_ref.dtype)


def norm_matmul(x, gain, w, *, tm, tn):
    t, d = x.shape
    n = w.shape[1]
    return pl.pallas_call(
        _norm_matmul_kernel,
        grid=(t // tm, n // tn),
        in_specs=[
            pl.BlockSpec((tm, d), lambda i, j: (i, 0)),
            pl.BlockSpec((1, d), lambda i, j: (0, 0)),
            pl.BlockSpec((d, tn), lambda i, j: (0, j)),
        ],
        out_specs=pl.BlockSpec((tm, tn), lambda i, j: (i, j)),
        out_shape=jax.ShapeDtypeStruct((t, n), BF16),
        scratch_shapes=[pltpu.VMEM((tm, d), BF16)],
        compiler_params=_cparams("parallel", "arbitrary"),
        name="norm_matmul",
    )(x, gain.reshape(1, d), w)


def _proj_residual_kernel(*refs):
    n_in = (len(refs) - 2) // 2
    res_ref, o_ref = refs[2 * n_in], refs[2 * n_in + 1]
    acc = res_ref[...]
    for i in range(n_in):
        acc = acc + _dot(refs[i][...], refs[n_in + i][...])
    o_ref[...] = acc


def proj_residual(acts, weights, res, *, tm):
    t, d = res.shape
    in_specs = [pl.BlockSpec((tm, a.shape[1]), lambda i: (i, 0)) for a in acts]
    in_specs += [pl.BlockSpec(w.shape, lambda i: (0, 0)) for w in weights]
    in_specs += [pl.BlockSpec((tm, d), lambda i: (i, 0))]
    return pl.pallas_call(
        _proj_residual_kernel,
        grid=(t // tm,),
        in_specs=in_specs,
        out_specs=pl.BlockSpec((tm, d), lambda i: (i, 0)),
        out_shape=jax.ShapeDtypeStruct((t, d), F32),
        compiler_params=_cparams("parallel"),
        name="proj_residual",
    )(*acts, *weights, res)


def _swiglu_kernel(x_ref, g_ref, wg_ref, wu_ref, wo_ref, o_ref, h_ref, acc_ref):
    c = pl.program_id(1)

    @pl.when(c == 0)
    def _():
        h_ref[...] = _rms(x_ref[...], g_ref[...]).astype(BF16)
        acc_ref[...] = jnp.zeros_like(acc_ref)

    h = h_ref[...]
    act = _silu(_dot(h, wg_ref[...])) * _dot(h, wu_ref[...])
    acc_ref[...] += _dot(act.astype(BF16), wo_ref[...])

    @pl.when(c == pl.num_programs(1) - 1)
    def _():
        o_ref[...] = x_ref[...] + acc_ref[...]


def swiglu(x, gain, w_in, w_out, *, tm, fc):
    t, d = x.shape
    f = w_out.shape[0]
    nf = f // fc
    return pl.pallas_call(
        _swiglu_kernel,
        grid=(t // tm, nf),
        in_specs=[
            pl.BlockSpec((tm, d), lambda i, c: (i, 0)),
            pl.BlockSpec((1, d), lambda i, c: (0, 0)),
            pl.BlockSpec((d, fc), lambda i, c: (0, c)),
            pl.BlockSpec((d, fc), lambda i, c: (0, c + nf)),
            pl.BlockSpec((fc, d), lambda i, c: (c, 0)),
        ],
        out_specs=pl.BlockSpec((tm, d), lambda i, c: (i, 0)),
        out_shape=jax.ShapeDtypeStruct((t, d), F32),
        scratch_shapes=[pltpu.VMEM((tm, d), BF16), pltpu.VMEM((tm, d), F32)],
        compiler_params=_cparams("parallel", "arbitrary"),
        name="swiglu_dense",
    )(x, gain.reshape(1, d), w_in, w_in, w_out)


def _half_rms(x, gain, low):
    sq = x * x
    s_lo = jnp.sum(jnp.where(low, sq, 0.0), axis=-1, keepdims=True)
    s_hi = jnp.sum(jnp.where(low, 0.0, sq), axis=-1, keepdims=True)
    ms = jnp.where(low, s_lo, s_hi) * (1.0 / DA_HEAD_DIM)
    return x * lax.rsqrt(ms + EPS) * gain


def _diff_attn_kernel(lam_ref, q_ref, k_ref, v_ref, qg_ref, kg_ref, sg_ref, o_ref,
                      kn_ref, vt_ref, s_ref, *, tq, lam_init):
    seq = q_ref.shape[0]
    low = lax.broadcasted_iota(jnp.int32, (1, LANES), 1) < DA_HEAD_DIM
    kn_ref[...] = _half_rms(k_ref[...].astype(F32), kg_ref[...], low).astype(BF16)
    vt_ref[...] = v_ref[...].astype(F32).T.astype(BF16)

    scale = DA_HEAD_DIM ** -0.5 * math.log2(math.e)
    key = lax.broadcasted_iota(jnp.int32, (tq, tq), 0)
    qry = lax.broadcasted_iota(jnp.int32, (tq, tq), 1)
    lam = lam_ref[0]

    for qi in range(seq // tq):
        rows = slice(qi * tq, (qi + 1) * tq)
        qn = _half_rms(q_ref[rows, :].astype(F32), qg_ref[...], low) * scale
        q_maps = (jnp.where(low, qn, 0.0).astype(BF16), jnp.where(low, 0.0, qn).astype(BF16))
        outs = []
        for c in range(2):
            m = jnp.full((1, tq), -jnp.inf, F32)
            for j in range(qi + 1):
                s = _dot_nt(kn_ref[j * tq:(j + 1) * tq, :], q_maps[c])
                if j == qi:
                    s = jnp.where(key <= qry, s, -jnp.inf)
                s_ref[c, j] = s
                m = jnp.maximum(m, jnp.max(s, axis=0, keepdims=True))
            l = jnp.zeros((1, tq), F32)
            acc = jnp.zeros((LANES, tq), F32)
            for j in range(qi + 1):
                p = jnp.exp2(s_ref[c, j] - m)
                l = l + jnp.sum(p, axis=0, keepdims=True)
                acc = acc + _dot(vt_ref[:, j * tq:(j + 1) * tq], p.astype(BF16))
            outs.append(acc / l)
        o = outs[0] - lam * outs[1]
        ms = jnp.mean(o * o, axis=0, keepdims=True)
        o = o * lax.rsqrt(ms + EPS) * sg_ref[...] * (1.0 - lam_init)
        o_ref[rows, :] = o.T.astype(o_ref.dtype)


def diff_attention(z, lam, q_gain, k_gain, subln_gain, *, seq, tq, lam_init):
    t = z.shape[0]
    b = t // seq
    nq = seq // tq
    qg = jnp.concatenate([q_gain, q_gain]).reshape(1, LANES)
    kg = jnp.concatenate([k_gain, k_gain]).reshape(1, LANES)
    sg = subln_gain.astype(F32).reshape(LANES, 1)
    vec = pl.BlockSpec((1, LANES), lambda bi, h: (0, 0))
    head = lambda off: pl.BlockSpec((seq, LANES), lambda bi, h: (bi, off + h))
    return pl.pallas_call(
        functools.partial(_diff_attn_kernel, tq=tq, lam_init=lam_init),
        grid=(b, DA_HEADS),
        in_specs=[pl.BlockSpec(memory_space=pltpu.SMEM),
                  head(0), head(DA_HEADS), head(2 * DA_HEADS),
                  vec, vec, pl.BlockSpec((LANES, 1), lambda bi, h: (0, 0))],
        out_specs=pl.BlockSpec((seq, LANES), lambda bi, h: (bi, h)),
        out_shape=jax.ShapeDtypeStruct((t, DA_HEADS * LANES), BF16),
        scratch_shapes=[pltpu.VMEM((seq, LANES), BF16),
                        pltpu.VMEM((LANES, seq), BF16),
                        pltpu.VMEM((2, nq, tq, tq), F32)],
        compiler_params=_cparams("parallel", "parallel"),
        name="diff_attention",
    )(lam.reshape(1), z, z, z, qg, kg, sg)


def _conv_kernel(val_ref, gate_ref, pval_ref, pgate_ref, dw_ref, b_ref, avg_ref, gg_ref, gb_ref,
                 o_ref, g_ref, sh_ref, *, tr):
    ti = pl.program_id(1)
    halo = pval_ref[...].astype(F32) * jax.nn.sigmoid(pgate_ref[...].astype(F32))
    g_ref[0:CONV_HALO, :] = jnp.where(ti == 0, 0.0, halo)
    g_ref[CONV_HALO:, :] = val_ref[...].astype(F32) * jax.nn.sigmoid(gate_ref[...].astype(F32))

    span = sh_ref.shape[1]
    for b in range(1, SUBLANES):
        sh_ref[b] = g_ref[b:b + span, :]

    first = CONV_HALO - (CONV_KERNEL - 1)
    y = jnp.broadcast_to(b_ref[...], o_ref.shape)
    for k in range(CONV_KERNEL):
        a, b = divmod(first + k, SUBLANES)
        src = g_ref if b == 0 else sh_ref.at[b]
        y = y + src[a * SUBLANES:a * SUBLANES + tr, :] * dw_ref[k:k + 1, :]

    avg = avg_ref[...]

    def group_mean(u):
        a, b, c = _split3(u)
        return _dot(a, avg) + _dot(b, avg) + _dot(c, avg)

    d = y - group_mean(y)
    var = group_mean(d * d)
    yn = d * lax.rsqrt(var + EPS) * gg_ref[...] + gb_ref[...]
    o_ref[...] = _silu(yn).astype(o_ref.dtype)


def conformer_conv(z, dw, dw_bias, gn_gain, gn_bias, *, seq, tr, col0):
    t = z.shape[0]
    c = dw.shape[1]
    b = t // seq
    nt = seq // tr
    hb = tr // CONV_HALO
    grp = c // CONV_GROUPS
    avg = jnp.asarray(np.kron(np.eye(CONV_GROUPS), np.full((grp, grp), 1.0 / grp)), BF16)
    cur = lambda off: pl.BlockSpec((tr, c), lambda bi, ti: (bi * nt + ti, col0 + off))
    prev = lambda off: pl.BlockSpec(
        (CONV_HALO, c), lambda bi, ti: (jnp.maximum((bi * nt + ti) * hb - 1, 0), col0 + off))
    vec = pl.BlockSpec((1, c), lambda bi, ti: (0, 0))
    return pl.pallas_call(
        functools.partial(_conv_kernel, tr=tr),
        grid=(b, nt),
        in_specs=[cur(0), cur(1), prev(0), prev(1),
                  pl.BlockSpec((CONV_KERNEL, c), lambda bi, ti: (0, 0)), vec,
                  pl.BlockSpec((c, c), lambda bi, ti: (0, 0)), vec, vec],
        out_specs=pl.BlockSpec((tr, c), lambda bi, ti: (bi * nt + ti, 0)),
        out_shape=jax.ShapeDtypeStruct((t, c), BF16),
        scratch_shapes=[pltpu.VMEM((CONV_HALO + tr, c), F32),
                        pltpu.VMEM((SUBLANES, CONV_HALO + tr - SUBLANES, c), F32)],
        compiler_params=_cparams("parallel", "arbitrary"),
        name="conformer_conv",
    )(z, z, z, z, dw, dw_bias.reshape(1, c), avg, gn_gain.reshape(1, c), gn_bias.reshape(1, c))


def _hgrn_level_halves(chunk):
    h = 1
    out = []
    while h < chunk:
        out.append(h)
        h *= 2
    return out


def _boundary_cum(cum, half):
    chunk, d = cum.shape
    if 2 * half >= SUBLANES:
        x = cum.reshape(chunk // (2 * half), 2 * half, d)
        return jnp.broadcast_to(x[:, half - 1:half, :], x.shape).reshape(chunk, d)
    x = cum.reshape(chunk // SUBLANES, SUBLANES, d)
    sub = lax.broadcasted_iota(jnp.int32, (1, SUBLANES, 1), 1)
    out = jnp.broadcast_to(x[:, half - 1:half, :], x.shape)
    for start in range(2 * half, SUBLANES, 2 * half):
        pick = jnp.broadcast_to(x[:, start + half - 1:start + half, :], x.shape)
        out = jnp.where(sub >= start, pick, out)
    return out.reshape(chunk, d)


def _block_diag(a, b):
    z = jnp.zeros_like(a)
    return jnp.concatenate([jnp.concatenate([a, z], axis=1), jnp.concatenate([z, b], axis=1)], axis=0)


def _right_rows_q_left_rows_k(q, k, half, row):
    if half >= SUBLANES:
        n = q.shape[0] // half
        return jnp.concatenate([(q if i % 2 else k)[i * half:(i + 1) * half] for i in range(n)], axis=0)
    return jnp.where((row & half) != 0, q, k)


def _hgrn_kernel(z_ref, tri_ref, loglb_ref, l1mlb_ref, ng_ref, o_ref, st_ref, *, ts, chunk):
    d = o_ref.shape[-1]
    dk = d // HGRN_HEADS

    @pl.when(pl.program_id(1) == 0)
    def _():
        st_ref[...] = jnp.zeros_like(st_ref)

    halves = _hgrn_level_halves(chunk)
    row = lax.broadcasted_iota(jnp.int32, (chunk, 1), 0)
    rr = lax.broadcasted_iota(jnp.int32, (chunk, 2 * chunk), 0)
    cc = lax.broadcasted_iota(jnp.int32, (chunk, 2 * chunk), 1) & (chunk - 1)
    diag = rr == cc
    level_mask = [((rr ^ cc) < 2 * half) & ((rr & half) != 0) & ((cc & half) == 0) for half in halves]
    tri3 = tri_ref[...]

    for ci in range(ts // chunk):
        r0 = ci * chunk
        f_pre = z_ref[r0:r0 + chunk, d:2 * d].astype(F32)
        log_sig = jnp.minimum(f_pre, 0.0) - jnp.log1p(jnp.exp(-jnp.abs(f_pre)))
        a = loglb_ref[...]
        b = l1mlb_ref[...] + log_sig
        mx = jnp.maximum(a, b)
        log_f = mx + jnp.log1p(jnp.exp(-jnp.abs(a - b)))
        f = jnp.exp(log_f)
        cum = _dot(tri3, jnp.concatenate(_split3(log_f), axis=0))
        last = cum[chunk - 1:chunk, :]
        k_all = -jnp.tanh(0.5 * log_f) * (f + 1.0)
        q_all = _silu(z_ref[r0:r0 + chunk, 0:d].astype(F32))

        level_ops = []
        for half in halves:
            if half == 1:
                w = jnp.where((row & half) != 0, f, 1.0)
            else:
                w = jnp.exp(-jnp.abs(cum - _boundary_cum(cum, half)))
            level_ops.append((_right_rows_q_left_rows_k(q_all, k_all, half, row) * w).astype(BF16))
        q_bf = q_all.astype(BF16)
        k_bf = k_all.astype(BF16)
        q_in = (q_all * jnp.exp(cum)).astype(BF16)
        k_out = (k_all * jnp.exp(last - cum)).astype(BF16)
        decay = jnp.exp(last)

        for hp in range(HGRN_HEADS // 2):
            l0 = slice(2 * hp * dk, (2 * hp + 1) * dk)
            l1 = slice((2 * hp + 1) * dk, (2 * hp + 2) * dk)
            l01 = slice(2 * hp * dk, (2 * hp + 2) * dk)
            scores = jnp.where(diag, _dot_nt(q_bf[:, l01], _block_diag(k_bf[:, l0], k_bf[:, l1])), 0.0)
            for u, mask in zip(level_ops, level_mask):
                scores = jnp.where(mask, _dot_nt(u[:, l01], _block_diag(u[:, l0], u[:, l1])), scores)
            v0 = z_ref[r0:r0 + chunk, 2 * d + 2 * hp * dk:2 * d + (2 * hp + 1) * dk]
            v1 = z_ref[r0:r0 + chunk, 2 * d + (2 * hp + 1) * dk:2 * d + (2 * hp + 2) * dk]
            st0 = st_ref[2 * hp]
            st1 = st_ref[2 * hp + 1]
            o = (_dot_nt(q_in[:, l01], _block_diag(st0.astype(BF16), st1.astype(BF16)))
                 + _dot(scores.astype(BF16), _block_diag(v0, v1)))
            st_ref[2 * hp] = st0 * decay[:, l0] + _dot_tn(v0, k_out[:, l0])
            st_ref[2 * hp + 1] = st1 * decay[:, l1] + _dot_tn(v1, k_out[:, l1])
            gate = _silu(z_ref[r0:r0 + chunk, 3 * d + 2 * hp * dk:3 * d + (2 * hp + 2) * dk].astype(F32))
            for i, ls in enumerate((l0, l1)):
                oh = o[:, i * dk:(i + 1) * dk]
                o_ref[r0:r0 + chunk, ls] = (_rms(oh, ng_ref[:, ls]) * gate[:, i * dk:(i + 1) * dk]
                                            ).astype(o_ref.dtype)


def hgrn2(z, lower_bound, norm_gain, *, seq, ts, chunk=HGRN_CHUNK):
    t = z.shape[0]
    d = z.shape[1] // 4
    b = t // seq
    nt = seq // ts
    assert chunk == d // HGRN_HEADS and chunk & (chunk - 1) == 0
    tri = np.tril(np.ones((chunk, chunk), np.float32))
    tri3 = jnp.asarray(np.concatenate([tri, tri, tri], axis=1), BF16)
    lb = lower_bound.astype(F32).reshape(1, d)
    vec = pl.BlockSpec((1, d), lambda bi, ti: (0, 0))
    return pl.pallas_call(
        functools.partial(_hgrn_kernel, ts=ts, chunk=chunk),
        grid=(b, nt),
        in_specs=[pl.BlockSpec((ts, 4 * d), lambda bi, ti: (bi * nt + ti, 0)),
                  pl.BlockSpec(tri3.shape, lambda bi, ti: (0, 0)), vec, vec, vec],
        out_specs=pl.BlockSpec((ts, d), lambda bi, ti: (bi * nt + ti, 0)),
        out_shape=jax.ShapeDtypeStruct((t, d), BF16),
        scratch_shapes=[pltpu.VMEM((HGRN_HEADS, d // HGRN_HEADS, d // HGRN_HEADS), F32)],
        compiler_params=_cparams("parallel", "arbitrary"),
        name="hgrn2",
    )(z, tri3, jnp.log(lb), jnp.log1p(-lb), norm_gain.reshape(1, d))


def _rows_from_tiles(buf, n, chunks):
    return jnp.concatenate(
        [buf[pl.ds(s, n, stride=SUBLANES), :] for s in range(chunks)], axis=-1)


def _rows_to_tiles(buf, val):
    n = val.shape[0]
    for s in range(val.shape[1] // LANES):
        buf[pl.ds(s, n, stride=SUBLANES), :] = val[:, s * LANES:(s + 1) * LANES]


def _router_kernel(x_ref, g_ref, r_ref, h_ref, info_ref):
    h = _rms(x_ref[...], g_ref[...])
    _rows_to_tiles(h_ref, h)
    h_hi, h_lo, _ = _split3(h)
    logits = _dot(h_hi, r_ref[0]) + _dot(h_hi, r_ref[1]) + _dot(h_lo, r_ref[0])
    lane = lax.broadcasted_iota(jnp.int32, logits.shape, 1).astype(F32)
    logits = jnp.where(lane < N_EXPERTS, logits, -jnp.inf)
    m0 = jnp.max(logits, axis=-1, keepdims=True)
    i0 = jnp.min(jnp.where(logits == m0, lane, float(LANES)), axis=-1, keepdims=True)
    rest = jnp.where(lane == i0, -jnp.inf, logits)
    m1 = jnp.max(rest, axis=-1, keepdims=True)
    i1 = jnp.min(jnp.where(rest == m1, lane, float(LANES)), axis=-1, keepdims=True)
    e1 = jnp.exp(m1 - m0)
    g0 = 1.0 / (1.0 + e1)
    g1 = e1 / (1.0 + e1)
    info = jnp.where(lane == 0.0, i0,
                     jnp.where(lane == 1.0, i1,
                               jnp.where(lane == 2.0, g0, jnp.where(lane == 3.0, g1, 0.0))))
    info_ref[...] = info


def router(x, gain, w_router, *, tm):
    t, d = x.shape
    wpad = jnp.zeros((d, LANES), F32).at[:, :N_EXPERTS].set(w_router.astype(F32))
    w2 = jnp.stack(_split3(wpad)[:2])
    return pl.pallas_call(
        _router_kernel,
        grid=(t // tm,),
        in_specs=[pl.BlockSpec((tm, d), lambda i: (i, 0)),
                  pl.BlockSpec((1, d), lambda i: (0, 0)),
                  pl.BlockSpec((2, d, LANES), lambda i: (0, 0, 0))],
        out_specs=[pl.BlockSpec((tm * SUBLANES, LANES), lambda i: (i, 0)),
                   pl.BlockSpec((tm, LANES), lambda i: (i, 0))],
        out_shape=[jax.ShapeDtypeStruct((t * SUBLANES, LANES), F32),
                   jax.ShapeDtypeStruct((t, LANES), F32)],
        compiler_params=_cparams("parallel"),
        name="router",
    )(x, gain.reshape(1, d), w2)


def _experts_kernel(eid_ref, nused_ref, cur_ref, nxt_ref, h_hbm, wg_ref, wu_ref, wo_ref, y_hbm,
                    xbuf, ybuf, h_ref, acc_ref, sem_x, sem_y, *, bm, gather_steps):
    del eid_ref
    b = pl.program_id(0)
    c = pl.program_id(1)
    last_c = pl.num_programs(1) - 1
    nused = nused_ref[0]
    slot = lax.rem(b, 2)
    per_step = bm // gather_steps
    chunks = h_ref.shape[1] // LANES

    def tile(ref, row):
        if isinstance(row, int):
            return ref.at[pl.ds(row * SUBLANES, SUBLANES), :]
        return ref.at[pl.ds(pl.multiple_of(row * SUBLANES, SUBLANES), SUBLANES), :]

    def for_rows(first, count, fn, unrolled):
        if unrolled:
            for i in range(count):
                fn(first + i, i % 2)
        else:
            def body(i, carry):
                for prio in range(2):
                    fn(first + 2 * i + prio, prio)
                return carry
            lax.fori_loop(0, count // 2, body, 0, unroll=4)

    def start_gather(idx_ref, dst_slot, first, count, unrolled):
        def one(r, prio):
            pltpu.make_async_copy(tile(h_hbm, idx_ref[0, 0, r]), tile(xbuf.at[dst_slot], r),
                                  sem_x.at[dst_slot]).start(priority=prio)
        for_rows(first, count, one, unrolled)

    def wait_gather(dst_slot):
        pltpu.make_async_copy(h_hbm.at[pl.ds(0, bm * SUBLANES), :], xbuf.at[dst_slot],
                              sem_x.at[dst_slot]).wait()

    def start_scatter(unrolled):
        def one(r, prio):
            pltpu.make_async_copy(tile(ybuf, r), tile(y_hbm, cur_ref[0, 0, bm + r]),
                                  sem_y).start(priority=prio)
        for_rows(0, bm, one, unrolled)

    def wait_scatter():
        pltpu.make_async_copy(ybuf, y_hbm.at[pl.ds(0, bm * SUBLANES), :], sem_y).wait()

    @pl.when(b < nused)
    def _():
        @pl.when(c == 0)
        def _():
            @pl.when(b == 0)
            def _():
                start_gather(cur_ref, 0, 0, bm, False)
            wait_gather(slot)
            h_ref[...] = _rows_from_tiles(xbuf.at[slot], bm, chunks).astype(BF16)
            acc_ref[...] = jnp.zeros_like(acc_ref)

        for step in range(gather_steps):
            @pl.when((c == step) & (b + 1 < nused))
            def _():
                start_gather(nxt_ref, 1 - slot, step * per_step, per_step, True)

        h = h_ref[...]
        act = _silu(_dot(h, wg_ref[...])) * _dot(h, wu_ref[...])
        acc_ref[...] += _dot(act.astype(BF16), wo_ref[...])

        @pl.when(c == last_c)
        def _():
            @pl.when(b > 0)
            def _():
                wait_scatter()
            _rows_to_tiles(ybuf, acc_ref[...])
            start_scatter(True)

            @pl.when(b == nused - 1)
            def _():
                wait_scatter()

    @pl.when((b >= nused) & (c == last_c))
    def _():
        start_scatter(False)
        wait_scatter()


def expert_swiglu(h_tiles, table, block_expert, nused, w_in, w_out, *, bm, fc, out_rows):
    nblk = table.shape[0]
    d = w_in.shape[1]
    f = w_out.shape[1]
    nf = f // fc
    gather_steps = min(4, nf)
    assert d == SUBLANES * LANES and bm % (2 * gather_steps) == 0 and f % fc == 0

    def chunk(b, c, nu):
        return jnp.where(b < nu[0], c, nf - 1)

    grid_spec = pltpu.PrefetchScalarGridSpec(
        num_scalar_prefetch=2,
        grid=(nblk, nf),
        in_specs=[
            pl.BlockSpec((1, 1, 2 * bm), lambda b, c, eid, nu: (b, 0, 0), memory_space=pltpu.SMEM),
            pl.BlockSpec((1, 1, 2 * bm), lambda b, c, eid, nu: (jnp.minimum(b + 1, nblk - 1), 0, 0),
                         memory_space=pltpu.SMEM),
            pl.BlockSpec(memory_space=pl.ANY),
            pl.BlockSpec((None, d, fc), lambda b, c, eid, nu: (eid[b], 0, chunk(b, c, nu))),
            pl.BlockSpec((None, d, fc), lambda b, c, eid, nu: (eid[b], 0, chunk(b, c, nu) + nf)),
            pl.BlockSpec((None, fc, d), lambda b, c, eid, nu: (eid[b], chunk(b, c, nu), 0)),
        ],
        out_specs=pl.BlockSpec(memory_space=pl.ANY),
        scratch_shapes=[pltpu.VMEM((2, bm * SUBLANES, LANES), F32),
                        pltpu.VMEM((bm * SUBLANES, LANES), F32),
                        pltpu.VMEM((bm, d), BF16),
                        pltpu.VMEM((bm, d), F32),
                        pltpu.SemaphoreType.DMA((2,)),
                        pltpu.SemaphoreType.DMA],
    )
    return pl.pallas_call(
        functools.partial(_experts_kernel, bm=bm, gather_steps=gather_steps),
        grid_spec=grid_spec,
        out_shape=jax.ShapeDtypeStruct((out_rows * SUBLANES, LANES), F32),
        compiler_params=_cparams("arbitrary", "arbitrary"),
        name="expert_swiglu",
    )(block_expert, nused, table, table, h_tiles, w_in, w_in, w_out)


def _combine_kernel(x_ref, y0_ref, y1_ref, g_ref, o_ref):
    tm = x_ref.shape[0]
    chunks = x_ref.shape[1] // LANES
    out = x_ref[...]
    for k, y_ref in enumerate((y0_ref, y1_ref)):
        out = out + g_ref[:, TOP_K + k:TOP_K + k + 1] * _rows_from_tiles(y_ref, tm, chunks)
    o_ref[...] = out


def combine(x, y_tiles, info, *, tm):
    t, d = x.shape
    nt = t // tm
    return pl.pallas_call(
        _combine_kernel,
        grid=(nt,),
        in_specs=[pl.BlockSpec((tm, d), lambda i: (i, 0)),
                  pl.BlockSpec((tm * SUBLANES, LANES), lambda i: (i, 0)),
                  pl.BlockSpec((tm * SUBLANES, LANES), lambda i: (i + nt, 0)),
                  pl.BlockSpec((tm, LANES), lambda i: (i, 0))],
        out_specs=pl.BlockSpec((tm, d), lambda i: (i, 0)),
        out_shape=jax.ShapeDtypeStruct((t, d), F32),
        compiler_params=_cparams("parallel"),
        name="moe_combine",
    )(x, y_tiles, y_tiles, info)


def _routing_tables(info, *, bm):
    t = info.shape[0]
    n_pairs = TOP_K * t
    experts = info[:, :TOP_K].astype(jnp.int32).T.reshape(-1)
    onehot = (experts[:, None] == jnp.arange(N_EXPERTS)[None, :]).astype(jnp.int32)
    rank = jnp.sum((jnp.cumsum(onehot, axis=0) - onehot) * onehot, axis=1)
    counts = jnp.sum(onehot, axis=0)
    padded = ((counts + bm - 1) // bm) * bm
    ends = jnp.cumsum(padded)
    starts_e = ends - padded
    slot = starts_e[experts] + rank
    n_pads = N_EXPERTS * bm
    n_slots = n_pairs + n_pads
    nblk = n_slots // bm
    pads_e = padded - counts
    pad_ends = jnp.cumsum(pads_e)
    i = jnp.arange(n_pads, dtype=jnp.int32)
    e_i = jnp.sum((i[:, None] >= pad_ends[None, :]).astype(jnp.int32), axis=1)
    e_c = jnp.minimum(e_i, N_EXPERTS - 1)
    pad_slot = jnp.where(e_i < N_EXPERTS,
                         (starts_e + counts)[e_c] + i - (pad_ends - pads_e)[e_c],
                         ends[-1] + i - pad_ends[-1])
    _, pair = lax.sort_key_val(jnp.concatenate([slot, pad_slot]),
                               jnp.concatenate([jnp.arange(n_pairs, dtype=jnp.int32), -1 - i]))
    is_pad = pair < 0
    token_row = jnp.where(is_pad, 0, pair % t)
    out_row = jnp.where(is_pad, n_pairs - 1 - pair, pair)
    table = jnp.concatenate([token_row.reshape(nblk, 1, bm), out_row.reshape(nblk, 1, bm)], axis=2)
    nused = (ends[-1] // bm).astype(jnp.int32)
    starts = jnp.arange(nblk, dtype=jnp.int32) * bm
    block_expert = jnp.sum((starts[:, None] >= ends[None, :]).astype(jnp.int32), axis=1)
    block_expert = jnp.minimum(block_expert, N_EXPERTS - 1)
    last_used = jnp.sum(jnp.where(jnp.arange(nblk) == nused - 1, block_expert, 0))
    block_expert = jnp.where(jnp.arange(nblk) < nused, block_expert, last_used).astype(jnp.int32)
    return table, block_expert, nused.reshape(1), n_slots


def moe_layer(x, gain, w_router, w_in, w_out, *, tm_route, bm, fc, tm_combine):
    h_tiles, info = router(x, gain, w_router, tm=tm_route)
    table, block_expert, nused, n_slots = _routing_tables(info, bm=bm)
    y_tiles = expert_swiglu(h_tiles, table, block_expert, nused, w_in, w_out,
                            bm=bm, fc=fc, out_rows=n_slots)
    return combine(x, y_tiles, info, tm=tm_combine)


def kernel(x, mix_norm, ffn_norm, ab_w_in, ab_w_out, da_q_gain, da_k_gain, da_lambda, da_subln,
           conv_dw, conv_dw_bias, conv_gn_gain, conv_gn_bias, hgrn_w_in, hgrn_lb_logits, hgrn_norm,
           hgrn_w_out, ffn_w_in, ffn_w_out, moe_router, moe_w_in, moe_w_out):
    batch, seq, d = x.shape
    depth = mix_norm.shape[0]
    attn_w = DA_HEADS * LANES
    xf = x.reshape(batch * seq, d)

    lb_all = jnp.cumsum(jax.nn.softmax(hgrn_lb_logits.astype(F32), axis=0), axis=0)
    lb_all = lb_all - lb_all[0:1]
    for layer in range(depth):
        j = layer // 2
        if layer % 2 == 0:
            z = norm_matmul(xf, mix_norm[layer], ab_w_in[j].astype(BF16), tm=1024, tn=1280)
            lam_init = 0.8 - 0.6 * math.exp(-0.3 * layer)
            lv = da_lambda[j].astype(F32)
            lam = jnp.exp(jnp.sum(lv[0] * lv[1])) - jnp.exp(jnp.sum(lv[2] * lv[3])) + lam_init
            a = diff_attention(z, lam, da_q_gain[j], da_k_gain[j], da_subln[j],
                               seq=seq, tq=512, lam_init=lam_init)
            c = conformer_conv(z, conv_dw[j], conv_dw_bias[j], conv_gn_gain[j], conv_gn_bias[j],
                               seq=seq, tr=512, col0=3)
            w_o = ab_w_out[j].astype(BF16)
            xf = proj_residual([a, c], [w_o[:attn_w], w_o[attn_w:]], xf, tm=1024)
            xf = swiglu(xf, ffn_norm[layer], ffn_w_in[j].astype(BF16), ffn_w_out[j].astype(BF16),
                        tm=512, fc=1408)
        else:
            z = norm_matmul(xf, mix_norm[layer], hgrn_w_in[j].astype(BF16), tm=1024, tn=2048)
            o = hgrn2(z, lb_all[j], hgrn_norm[j], seq=seq, ts=256)
            xf = proj_residual([o], [hgrn_w_out[j].astype(BF16)], xf, tm=1024)
            xf = moe_layer(xf, ffn_norm[layer], moe_router[j], moe_w_in[j].astype(BF16),
                           moe_w_out[j].astype(BF16), tm_route=1024, bm=1024, fc=512,
                           tm_combine=256)
    return xf.reshape(batch, seq, d)
```
